```python
import math
import jax, jax.numpy as jnp
from jax import lax
import numpy as np

D_MODEL = 1024
BATCH = 8
SEQ = 2048
DEPTH = 1

DIFF_HEADS = 4
DIFF_QK_DIM = 64
DIFF_V_DIM = 2 * DIFF_QK_DIM
DIFF_WIDTH = DIFF_HEADS * DIFF_V_DIM
MOBA_HEADS = 8
MOBA_HEAD_DIM = 64
MOBA_WIDTH = MOBA_HEADS * MOBA_HEAD_DIM
MOBA_BLOCK = 256
MOBA_TOPK = 3
MOBA_Q_CHUNK = 16
MIX_WIDTH = DIFF_WIDTH + MOBA_WIDTH
IN_WIDTH = 4 * DIFF_WIDTH + 4 * MOBA_WIDTH
N_BUCKETS = 32
MAX_DISTANCE = 128
N_BIAS_HEADS = DIFF_HEADS + MOBA_HEADS
Q_BLOCK = 128
NORM_EPS = 1e-6
SUBLN_EPS = 1e-5
NEG = -1e30

kernel_name = "hymba_diff_moba_hybrid"


def rmsnorm(x, g, eps=NORM_EPS):
    xf = x.astype(jnp.float32)
    ms = jnp.mean(xf * xf, axis=-1, keepdims=True)
    return (xf * lax.rsqrt(ms + eps) * g.astype(jnp.float32)).astype(x.dtype)


def rel_bucket(dist):
    n = jnp.maximum(dist, 0)
    max_exact = N_BUCKETS // 2
    nf = jnp.maximum(n, 1).astype(jnp.float32)
    large = max_exact + (jnp.log(nf / max_exact) / math.log(MAX_DISTANCE / max_exact)
                         * (N_BUCKETS - max_exact)).astype(jnp.int32)
    large = jnp.minimum(large, N_BUCKETS - 1)
    return jnp.where(n < max_exact, n, large)


def diff_attention(q, k, v, lam, lambda_init, subln_g, bias_tab):
    B, H, S = q.shape[0], q.shape[1], q.shape[2]
    scale = DIFF_QK_DIM ** -0.5
    nqb = S // Q_BLOCK
    q_blocks = jnp.moveaxis(q.reshape(B, H, nqb, Q_BLOCK, 2, DIFF_QK_DIM), 2, 0)
    k_pos = jnp.arange(S)

    def one_block(args):
        qb, blk = args
        q_pos = blk * Q_BLOCK + jnp.arange(Q_BLOCK)
        dist = q_pos[:, None] - k_pos[None, :]
        bias = jnp.moveaxis(bias_tab[rel_bucket(dist)], -1, 0)
        logits = jnp.einsum('bhqcd,bhkcd->bhcqk', qb, k).astype(jnp.float32) * scale
        logits = logits + bias[None, :, None].astype(jnp.float32)
        logits = jnp.where(dist >= 0, logits, -jnp.inf)
        p = jax.nn.softmax(logits, axis=-1)
        w = p[:, :, 0] - lam * p[:, :, 1]
        return jnp.einsum('bhqk,bhkd->bhqd', w.astype(v.dtype), v)

    out = lax.map(one_block, (q_blocks, jnp.arange(nqb)))
    out = jnp.moveaxis(out, 0, 2).reshape(B, H, S, DIFF_V_DIM)
    return rmsnorm(out, subln_g, SUBLN_EPS) * (1.0 - lambda_init)


def moba_attention(q, k, v, bias_tab):
    B, H, S, D = q.shape
    scale = D ** -0.5
    nb = -(-S // MOBA_BLOCK)
    pad = nb * MOBA_BLOCK - S
    kp = jnp.pad(k, ((0, 0), (0, 0), (0, pad), (0, 0)))
    vp = jnp.pad(v, ((0, 0), (0, 0), (0, pad), (0, 0)))
    kb = kp.reshape(B, H, nb, MOBA_BLOCK, D)
    vb = vp.reshape(B, H, nb, MOBA_BLOCK, D)
    k_mean = jnp.mean(kb.astype(jnp.float32), axis=3)
    top = max(1, min(MOBA_TOPK, nb - 1))
    bias_flat = bias_tab.T.reshape(-1)
    head_off = (jnp.arange(H) * N_BUCKETS)[None, :, None, None, None]
    nc = S // MOBA_Q_CHUNK
    q_chunks = jnp.moveaxis(q.reshape(B, H, nc, MOBA_Q_CHUNK, D), 2, 0)
    gather_blocks = jax.vmap(jax.vmap(lambda blocks, idx: blocks[idx]))
    blk_off = jnp.arange(MOBA_BLOCK)

    def one_chunk(args):
        qb, ci = args
        t0 = ci * MOBA_Q_CHUNK
        q_pos = t0 + jnp.arange(MOBA_Q_CHUNK)
        cur = t0 // MOBA_BLOCK
        gate = jnp.einsum('bhqd,bhnd->bhqn', qb.astype(jnp.float32), k_mean)
        gate = jnp.where(jnp.arange(nb) < cur, gate, NEG)
        _, idx = lax.top_k(gate, top)
        valid = idx < cur
        k_sel = gather_blocks(kb, idx)
        v_sel = gather_blocks(vb, idx)
        k_sel_pos = idx[..., None] * MOBA_BLOCK + blk_off
        s_sel = jnp.einsum('bhqd,bhqtkd->bhqtk', qb, k_sel).astype(jnp.float32) * scale
        dist_sel = q_pos[:, None, None] - k_sel_pos
        s_sel = s_sel + bias_flat[head_off + rel_bucket(dist_sel)].astype(jnp.float32)
        s_sel = jnp.where(valid[..., None], s_sel, -jnp.inf)
        s_sel = s_sel.reshape(B, H, MOBA_Q_CHUNK, top * MOBA_BLOCK)
        k_own = lax.dynamic_slice_in_dim(kb, cur, 1, axis=2)[:, :, 0]
        v_own = lax.dynamic_slice_in_dim(vb, cur, 1, axis=2)[:, :, 0]
        dist_own = q_pos[:, None] - (cur * MOBA_BLOCK + blk_off)[None, :]
        bias_own = jnp.moveaxis(bias_tab[rel_bucket(dist_own)], -1, 0)
        s_own = jnp.einsum('bhqd,bhkd->bhqk', qb, k_own).astype(jnp.float32) * scale
        s_own = s_own + bias_own[None].astype(jnp.float32)
        s_own = jnp.where(dist_own >= 0, s_own, -jnp.inf)
        p = jax.nn.softmax(jnp.concatenate([s_sel, s_own], axis=-1), axis=-1)
        p_sel = p[..., :top * MOBA_BLOCK].reshape(B, H, MOBA_Q_CHUNK, top, MOBA_BLOCK)
        p_own = p[..., top * MOBA_BLOCK:]
        return (jnp.einsum('bhqtk,bhqtkd->bhqd', p_sel.astype(v.dtype), v_sel)
                + jnp.einsum('bhqk,bhkd->bhqd', p_own.astype(v.dtype), v_own))

    out = lax.map(one_chunk, (q_chunks, jnp.arange(nc)))
    return jnp.moveaxis(out, 0, 2).reshape(B, H, S, D)


def setup_inputs(seed: int = 0) -> dict:
    key = jax.random.key(seed)
    ks = jax.random.split(key, 8)
    x = jax.random.normal(ks[0], (BATCH, SEQ, D_MODEL), jnp.float32)
    norm_pre_g = 1.0 + 0.02 * jax.random.normal(ks[1], (DEPTH, D_MODEL), jnp.float32)
    w_in = jax.random.normal(ks[2], (DEPTH, D_MODEL, IN_WIDTH), jnp.float32) * D_MODEL ** -0.5
    diff_lambda = 0.1 * jax.random.normal(ks[3], (DEPTH, 4, DIFF_QK_DIM), jnp.float32)
    diff_subln_g = 1.0 + 0.02 * jax.random.normal(ks[4], (DEPTH, DIFF_V_DIM), jnp.float32)
    w_out = jax.random.normal(ks[5], (DEPTH, MIX_WIDTH, D_MODEL), jnp.float32) * MIX_WIDTH ** -0.5
    norm_post_g = 1.0 + 0.02 * jax.random.normal(ks[6], (DEPTH, D_MODEL), jnp.float32)
    rel_bias = 0.2 * jax.random.normal(ks[7], (N_BUCKETS, N_BIAS_HEADS), jnp.float32)
    return {"x": x, "norm_pre_g": norm_pre_g, "w_in": w_in, "diff_lambda": diff_lambda,
            "diff_subln_g": diff_subln_g, "w_out": w_out, "norm_post_g": norm_post_g,
            "rel_bias": rel_bias}


def reference(x, norm_pre_g, w_in, diff_lambda, diff_subln_g, w_out, norm_post_g, rel_bias):
    B, S, _ = x.shape
    splits = [DIFF_WIDTH, 2 * DIFF_WIDTH, 3 * DIFF_WIDTH, 4 * DIFF_WIDTH,
              4 * DIFF_WIDTH + MOBA_WIDTH, 4 * DIFF_WIDTH + 2 * MOBA_WIDTH,
              4 * DIFF_WIDTH + 3 * MOBA_WIDTH]
    bias_diff = rel_bias[:, :DIFF_HEADS]
    bias_moba = rel_bias[:, DIFF_HEADS:]
    for layer in range(DEPTH):
        h = rmsnorm(x, norm_pre_g[layer])
        u = jnp.einsum('bsd,de->bse', h, w_in[layer])
        dq, dk, dv, dg, mq, mk, mv, mg = jnp.split(u, splits, axis=-1)
        lambda_init = 0.8 - 0.6 * math.exp(-0.3 * layer)
        lp = diff_lambda[layer].astype(jnp.float32)
        lam = jnp.exp(jnp.sum(lp[0] * lp[1])) - jnp.exp(jnp.sum(lp[2] * lp[3])) + lambda_init
        to_qk = lambda t: jnp.transpose(t.reshape(B, S, DIFF_HEADS, 2, DIFF_QK_DIM), (0, 2, 1, 3, 4))
        dvh = jnp.transpose(dv.reshape(B, S, DIFF_HEADS, DIFF_V_DIM), (0, 2, 1, 3))
        a_out = diff_attention(to_qk(dq), to_qk(dk), dvh, lam, lambda_init,
                               diff_subln_g[layer], bias_diff)
        a_out = jnp.transpose(a_out, (0, 2, 1, 3)).reshape(B, S, DIFF_WIDTH) * jax.nn.silu(dg)
        to_h = lambda t: jnp.transpose(t.reshape(B, S, MOBA_HEADS, MOBA_HEAD_DIM), (0, 2, 1, 3))
        b_out = moba_attention(to_h(mq), to_h(mk), to_h(mv), bias_moba)
        b_out = jnp.transpose(b_out, (0, 2, 1, 3)).reshape(B, S, MOBA_WIDTH) * jax.nn.silu(mg)
        y = jnp.einsum('bse,ed->bsd', jnp.concatenate([a_out, b_out], axis=-1), w_out[layer])
        x = x + rmsnorm(y, norm_post_g[layer])
    return x
```

```python
import functools
import math

import numpy as np
import jax
import jax.numpy as jnp
from jax import lax
from jax.experimental import pallas as pl
from jax.experimental.pallas import tpu as pltpu

F32 = jnp.float32
BF16 = jnp.bfloat16

D_MODEL = 1024
DIFF_HEADS = 4
DIFF_QK = 64
MOBA_HEADS = 8
MOBA_DIM = 64
MOBA_TOPK = 3
N_BUCKETS = 32
MAX_DISTANCE = 128
N_BIAS_HEADS = DIFF_HEADS + MOBA_HEADS
IN_WIDTH = 4096
GROUP = 512
BLK = 256
LANES = 128
NORM_EPS = 1e-6
SUBLN_EPS = 1e-5
QK_SCALE = 0.125
MASK_BIG = 1e30
VMEM_LIMIT = 56 * 1024 * 1024

_NT = (((1,), (1,)), ((), ()))


def _bucket_tiles():
    i = np.arange(BLK)[:, None]
    j = np.arange(BLK)[None, :]
    out = []
    for base in (0, BLK):
        n = np.maximum(base + i - j, 0)
        max_exact = N_BUCKETS // 2
        nf = np.maximum(n, 1).astype(np.float32)
        ratio = np.log(nf / np.float32(max_exact)) / np.float32(math.log(MAX_DISTANCE / max_exact))
        large = max_exact + (ratio * np.float32(N_BUCKETS - max_exact)).astype(np.int32)
        large = np.minimum(large, N_BUCKETS - 1)
        out.append(np.where(n < max_exact, n, large).astype(np.int32))
    return np.stack(out)


def _inproj_kernel(x_ref, g_ref, w_ref, u_ref):
    x = x_ref[...]
    ms = jnp.mean(x * x, axis=-1, keepdims=True)
    h = (x * lax.rsqrt(ms + NORM_EPS) * g_ref[...]).astype(BF16)
    for j in range(IN_WIDTH // GROUP):
        acc = jnp.dot(h, w_ref[:, j * GROUP:(j + 1) * GROUP], preferred_element_type=F32)
        if j in (0, 4):
            acc = acc * QK_SCALE
        u_ref[:, j * GROUP:(j + 1) * GROUP] = acc.astype(BF16)


def _inproj(x2d, g, w_bf16, tm=512):
    m = x2d.shape[0]
    return pl.pallas_call(
        _inproj_kernel,
        grid=(m // tm,),
        in_specs=[
            pl.BlockSpec((tm, D_MODEL), lambda i: (i, 0)),
            pl.BlockSpec((1, D_MODEL), lambda i: (0, 0)),
            pl.BlockSpec((D_MODEL, IN_WIDTH), lambda i: (0, 0)),
        ],
        out_specs=pl.BlockSpec((tm, IN_WIDTH), lambda i: (i, 0)),
        out_shape=jax.ShapeDtypeStruct((m, IN_WIDTH), BF16),
        compiler_params=pltpu.CompilerParams(
            dimension_semantics=("arbitrary",), vmem_limit_bytes=VMEM_LIMIT),
        name="inproj",
    )(x2d, g, w_bf16)


def _bias_kernel(tab_ref, bkt_ref, out_ref):
    h = pl.program_id(0)
    row = lax.broadcasted_iota(jnp.int32, (BLK, BLK), 0)
    col = lax.broadcasted_iota(jnp.int32, (BLK, BLK), 1)
    for t in range(2):
        bk = bkt_ref[t]
        acc = jnp.zeros((BLK, BLK), F32)
        for b in range(N_BUCKETS):
            acc = jnp.where(bk == b, tab_ref[b, h], acc)
        if t == 0:
            acc = jnp.where(row >= col, acc, -jnp.inf)
        out_ref[0, t] = acc
    out_ref[0, 2] = jnp.full((BLK, BLK), tab_ref[N_BUCKETS - 1, h], F32)


def _bias_tiles(rel_bias):
    bkt = jnp.asarray(_bucket_tiles())
    return pl.pallas_call(
        _bias_kernel,
        grid=(N_BIAS_HEADS,),
        in_specs=[
            pl.BlockSpec(memory_space=pltpu.SMEM),
            pl.BlockSpec((2, BLK, BLK), lambda h: (0, 0, 0)),
        ],
        out_specs=pl.BlockSpec((1, 3, BLK, BLK), lambda h: (h, 0, 0, 0)),
        out_shape=jax.ShapeDtypeStruct((N_BIAS_HEADS, 3, BLK, BLK), F32),
        compiler_params=pltpu.CompilerParams(dimension_semantics=("arbitrary",)),
        name="bias_tiles",
    )(rel_bias, bkt)


def _two_problem_attention(qi, q_ext, k_at, v_at, bias_at, s_scr, m_scr, l_scr, acc_scr):
    m_scr[...] = jnp.full(m_scr.shape, -jnp.inf, F32)

    def scores(kb, carry):
        kblk = k_at(kb)
        idx = jnp.minimum(qi - kb, 2)
        for c in range(2):
            s = lax.dot_general(q_ext[c], kblk, _NT, preferred_element_type=F32)
            s = s + bias_at(c, idx)
            s_scr[c, kb] = s
            m_scr[c] = jnp.maximum(m_scr[c], jnp.maximum(s[:, :LANES], s[:, LANES:]))
        return carry

    lax.fori_loop(0, qi + 1, scores, 0)

    for c in range(2):
        m_row = jnp.max(m_scr[c], axis=-1, keepdims=True)
        m_scr[c] = jnp.broadcast_to(m_row, (BLK, LANES))
    l_scr[...] = jnp.zeros(l_scr.shape, F32)
    acc_scr[...] = jnp.zeros(acc_scr.shape, F32)

    def weighted(kb, carry):
        vblk = v_at(kb)
        for c in range(2):
            mb = m_scr[c]
            s = s_scr[c, kb]
            p_lo = jnp.exp(s[:, :LANES] - mb)
            p_hi = jnp.exp(s[:, LANES:] - mb)
            l_scr[c] += p_lo + p_hi
            p = jnp.concatenate([p_lo, p_hi], axis=1).astype(BF16)
            acc_scr[c] += jnp.dot(p, vblk, preferred_element_type=F32)
        return carry

    lax.fori_loop(0, qi + 1, weighted, 0)

    outs = []
    for c in range(2):
        l_row = jnp.sum(l_scr[c], axis=-1, keepdims=True)
        outs.append(acc_scr[c] / l_row)
    return outs


def _silu(g):
    return g / (1.0 + jnp.exp(-g))


def _diff_kernel(q_ref, k_ref, v_ref, gate_ref, bias_ref, lam_ref, sg_ref, o_ref,
                 s_scr, m_scr, l_scr, acc_scr):
    qi = pl.program_id(2)
    q = q_ref[0, 0].astype(F32)
    lane = lax.broadcasted_iota(jnp.int32, (BLK, LANES), 1)
    q_ext = [jnp.where(lane < DIFF_QK, q, 0.0).astype(BF16),
             jnp.where(lane >= DIFF_QK, q, 0.0).astype(BF16)]
    o0, o1 = _two_problem_attention(
        qi, q_ext,
        lambda kb: k_ref[0, kb], lambda kb: v_ref[0, kb],
        lambda c, idx: bias_ref[0, idx],
        s_scr, m_scr, l_scr, acc_scr)

    lp = lam_ref[...]
    lambda_init = 0.8 - 0.6 * math.exp(-0.3 * 0)
    lam = (jnp.exp(jnp.sum(lp[0:1] * lp[1:2], axis=-1, keepdims=True))
           - jnp.exp(jnp.sum(lp[2:3] * lp[3:4], axis=-1, keepdims=True)) + lambda_init)
    out = o0 - lam * o1
    ms = jnp.mean(out * out, axis=-1, keepdims=True)
    out = out * lax.rsqrt(ms + SUBLN_EPS) * sg_ref[...] * (1.0 - lambda_init)
    out = out * _silu(gate_ref[0, 0].astype(F32))
    o_ref[...] = out.astype(BF16)


def _diff_attention(u4, bias, diff_lambda, subln_g):
    b, nblk = u4.shape[0], u4.shape[1]
    grp = GROUP // LANES
    return pl.pallas_call(
        _diff_kernel,
        grid=(b, DIFF_HEADS, nblk),
        in_specs=[
            pl.BlockSpec((1, 1, BLK, LANES), lambda bi, h, qi: (bi, qi, 0, h)),
            pl.BlockSpec((1, nblk, BLK, LANES), lambda bi, h, qi: (bi, 0, 0, grp + h)),
            pl.BlockSpec((1, nblk, BLK, LANES), lambda bi, h, qi: (bi, 0, 0, 2 * grp + h)),
            pl.BlockSpec((1, 1, BLK, LANES), lambda bi, h, qi: (bi, qi, 0, 3 * grp + h)),
            pl.BlockSpec((1, 3, BLK, BLK), lambda bi, h, qi: (h, 0, 0, 0)),
            pl.BlockSpec((4, DIFF_QK), lambda bi, h, qi: (0, 0)),
            pl.BlockSpec((1, LANES), lambda bi, h, qi: (0, 0)),
        ],
        out_specs=pl.BlockSpec((BLK, LANES), lambda bi, h, qi: (bi * nblk + qi, h)),
        out_shape=jax.ShapeDtypeStruct((b * nblk * BLK, DIFF_HEADS * LANES), BF16),
        scratch_shapes=[
            pltpu.VMEM((2, nblk, BLK, BLK), F32),
            pltpu.VMEM((2, BLK, LANES), F32),
            pltpu.VMEM((2, BLK, LANES), F32),
            pltpu.VMEM((2, BLK, LANES), F32),
        ],
        compiler_params=pltpu.CompilerParams(
            dimension_semantics=("arbitrary", "arbitrary", "arbitrary"),
            vmem_limit_bytes=VMEM_LIMIT),
        name="diff_attention",
    )(u4, u4, u4, u4, bias, diff_lambda, subln_g)


def _moba_kernel(q_ref, k_ref, v_ref, gate_ref, bias_a_ref, bias_b_ref, o_ref,
                 kext_scr, wg_scr, s_scr, m_scr, l_scr, acc_scr):
    qi = pl.program_id(2)
    nblk = k_ref.shape[1]

    @pl.when(qi == 0)
    def _prepare_keys():
        lane = lax.broadcasted_iota(jnp.int32, (BLK, LANES), 1)
        wrow = lax.broadcasted_iota(jnp.int32, (LANES, LANES), 0)
        wlane = lax.broadcasted_iota(jnp.int32, (LANES, LANES), 1)
        wg = jnp.zeros((LANES, LANES), F32)
        for n in range(nblk):
            kblk = k_ref[0, n]
            kext_scr[n, :, :LANES] = kblk
            onehot = jnp.where((lane == n) | (lane == nblk + n), 1.0, 0.0)
            kext_scr[n, :, LANES:] = onehot.astype(BF16)
            kmean = jnp.sum(kblk.astype(F32), axis=0, keepdims=True) * (1.0 / BLK)
            hi = kmean.astype(BF16).astype(F32)
            mid = (kmean - hi).astype(BF16).astype(F32)
            lo = (kmean - hi - mid).astype(BF16).astype(F32)
            for piece, val in enumerate((hi, mid, lo)):
                for hh in range(2):
                    r = piece * 2 * nblk + hh * nblk + n
                    head_lanes = (wlane >= hh * MOBA_DIM) & (wlane < (hh + 1) * MOBA_DIM)
                    wg = jnp.where((wrow == r) & head_lanes, val, wg)
        wg_scr[...] = wg.astype(BF16)

    q = q_ref[0, 0]
    g_t = lax.dot_general(wg_scr[...], q, _NT, preferred_element_type=F32)
    g_t = g_t[0:2 * nblk] + g_t[2 * nblk:4 * nblk] + g_t[4 * nblk:6 * nblk]
    n_iota = lax.broadcasted_iota(jnp.int32, (nblk, BLK), 0)
    mask_rows = []
    for hh in range(2):
        g8 = g_t[hh * nblk:(hh + 1) * nblk]
        cnt = jnp.zeros((nblk, BLK), jnp.int32)
        for i in range(nblk):
            gi = g8[i:i + 1]
            beats = (gi > g8) | ((gi == g8) & (i < n_iota))
            cnt = cnt + jnp.where(beats & (i < qi), 1, 0)
        keep = ((n_iota < qi) & (cnt < MOBA_TOPK)) | (n_iota == qi)
        mask_rows.append(jnp.where(keep, 0.0, -MASK_BIG))
    mask_t = jnp.concatenate(mask_rows + [jnp.zeros((LANES - 2 * nblk, BLK), F32)], axis=0)
    mask = mask_t.T

    lane = lax.broadcasted_iota(jnp.int32, (BLK, LANES), 1)
    qf = q.astype(F32)
    q_ext = []
    for hh in range(2):
        head_lanes = (lane >= hh * MOBA_DIM) & (lane < (hh + 1) * MOBA_DIM)
        mask_lanes = (lane >= hh * nblk) & (lane < (hh + 1) * nblk)
        q_ext.append(jnp.concatenate(
            [jnp.where(head_lanes, qf, 0.0), jnp.where(mask_lanes, mask, 0.0)], axis=1).astype(BF16))

    bias_refs = (bias_a_ref, bias_b_ref)
    o_a, o_b = _two_problem_attention(
        qi, q_ext,
        lambda kb: kext_scr[kb], lambda kb: v_ref[0, kb],
        lambda c, idx: bias_refs[c][0, idx],
        s_scr, m_scr, l_scr, acc_scr)
    out = jnp.where(lane < MOBA_DIM, o_a, o_b)
    out = out * _silu(gate_ref[0, 0].astype(F32))
    o_ref[...] = out.astype(BF16)


def _moba_attention(u4, bias):
    b, nblk = u4.shape[0], u4.shape[1]
    grp = GROUP // LANES
    pairs = MOBA_HEADS // 2
    return pl.pallas_call(
        _moba_kernel,
        grid=(b, pairs, nblk),
        in_specs=[
            pl.BlockSpec((1, 1, BLK, LANES), lambda bi, p, qi: (bi, qi, 0, 4 * grp + p)),
            pl.BlockSpec((1, nblk, BLK, LANES), lambda bi, p, qi: (bi, 0, 0, 5 * grp + p)),
            pl.BlockSpec((1, nblk, BLK, LANES), lambda bi, p, qi: (bi, 0, 0, 6 * grp + p)),
            pl.BlockSpec((1, 1, BLK, LANES), lambda bi, p, qi: (bi, qi, 0, 7 * grp + p)),
            pl.BlockSpec((1, 3, BLK, BLK), lambda bi, p, qi: (DIFF_HEADS + 2 * p, 0, 0, 0)),
            pl.BlockSpec((1, 3, BLK, BLK), lambda bi, p, qi: (DIFF_HEADS + 2 * p + 1, 0, 0, 0)),
        ],
        out_specs=pl.BlockSpec((BLK, LANES), lambda bi, p, qi: (bi * nblk + qi, p)),
        out_shape=jax.ShapeDtypeStruct((b * nblk * BLK, pairs * LANES), BF16),
        scratch_shapes=[
            pltpu.VMEM((nblk, BLK, 2 * LANES), BF16),
            pltpu.VMEM((LANES, LANES), BF16),
            pltpu.VMEM((2, nblk, BLK, BLK), F32),
            pltpu.VMEM((2, BLK, LANES), F32),
            pltpu.VMEM((2, BLK, LANES), F32),
            pltpu.VMEM((2, BLK, LANES), F32),
        ],
        compiler_params=pltpu.CompilerParams(
            dimension_semantics=("arbitrary", "arbitrary", "arbitrary"),
            vmem_limit_bytes=VMEM_LIMIT),
        name="moba_attention",
    )(u4, u4, u4, u4, bias, bias)


def _outproj_kernel(a_ref, b_ref, x_ref, g_ref, w_ref, o_ref):
    half = a_ref.shape[1]
    y = jnp.dot(a_ref[...], w_ref[:half, :], preferred_element_type=F32)
    y = y + jnp.dot(b_ref[...], w_ref[half:, :], preferred_element_type=F32)
    ms = jnp.mean(y * y, axis=-1, keepdims=True)
    o_ref[...] = x_ref[...] + y * lax.rsqrt(ms + NORM_EPS) * g_ref[...]


def _outproj(a, b, x2d, g, w_bf16, tm=512):
    m = x2d.shape[0]
    half = a.shape[1]
    return pl.pallas_call(
        _outproj_kernel,
        grid=(m // tm,),
        in_specs=[
            pl.BlockSpec((tm, half), lambda i: (i, 0)),
            pl.BlockSpec((tm, half), lambda i: (i, 0)),
            pl.BlockSpec((tm, D_MODEL), lambda i: (i, 0)),
            pl.BlockSpec((1, D_MODEL), lambda i: (0, 0)),
            pl.BlockSpec((2 * half, D_MODEL), lambda i: (0, 0)),
        ],
        out_specs=pl.BlockSpec((tm, D_MODEL), lambda i: (i, 0)),
        out_shape=jax.ShapeDtypeStruct((m, D_MODEL), F32),
        compiler_params=pltpu.CompilerParams(
            dimension_semantics=("arbitrary",), vmem_limit_bytes=VMEM_LIMIT),
        name="outproj",
    )(a, b, x2d, g, w_bf16)


def kernel(x, norm_pre_g, w_in, diff_lambda, diff_subln_g, w_out, norm_post_g, rel_bias):
    b, s, d = x.shape
    assert d == D_MODEL and s % BLK == 0
    assert norm_pre_g.shape[0] == 1, "single-layer block"
    x2d = x.reshape(b * s, d)
    u = _inproj(x2d, norm_pre_g[0:1], w_in[0].astype(BF16))
    u4 = u.reshape(b, s // BLK, BLK, IN_WIDTH)
    bias = _bias_tiles(rel_bias)
    a_out = _diff_attention(u4, bias, diff_lambda[0], diff_subln_g[0:1])
    b_out = _moba_attention(u4, bias)
    out = _outproj(a_out, b_out, x2d, norm_post_g[0:1], w_out[0].astype(BF16))
    return out.reshape(b, s, d)
```

```python
import math

import numpy as np
import jax
import jax.numpy as jnp
from jax import lax
from jax.experimental import pallas as pl
from jax.experimental.pallas import tpu as pltpu

F32 = jnp.float32
BF16 = jnp.bfloat16

D_MODEL = 1024
DIFF_HEADS = 4
DIFF_QK = 64
MOBA_HEADS = 8
MOBA_DIM = 64
MOBA_TOPK = 3
N_BUCKETS = 32
MAX_DISTANCE = 128
IN_WIDTH = 4096
GROUP = 512
BLK = 256
NBLK = 8
NPAIR = NBLK // 2
NSLOT = NBLK + 1
LANES = 128
NORM_EPS = 1e-6
SUBLN_EPS = 1e-5
QK_SCALE = 0.125
MASK_BIG = 1e30
VMEM_LIMIT = 56 * 1024 * 1024

_NT = (((1,), (1,)), ((), ()))


def _bucket_tiles():
    i = np.arange(BLK)[:, None]
    j = np.arange(BLK)[None, :]
    out = []
    for base in (0, BLK):
        n = np.maximum(base + i - j, 0)
        max_exact = N_BUCKETS // 2
        nf = np.maximum(n, 1).astype(np.float32)
        ratio = np.log(nf / np.float32(max_exact)) / np.float32(math.log(MAX_DISTANCE / max_exact))
        large = max_exact + (ratio * np.float32(N_BUCKETS - max_exact)).astype(np.int32)
        large = np.minimum(large, N_BUCKETS - 1)
        out.append(np.where(n < max_exact, n, large).astype(np.int32))
    return np.stack(out)


def _inproj_kernel(x_ref, g_ref, w_ref, u_ref):
    x = x_ref[...]
    ms = jnp.mean(x * x, axis=-1, keepdims=True)
    h = (x * lax.rsqrt(ms + NORM_EPS) * g_ref[...]).astype(BF16)
    for j in range(IN_WIDTH // GROUP):
        acc = jnp.dot(h, w_ref[:, j * GROUP:(j + 1) * GROUP], preferred_element_type=F32)
        if j in (0, 4):
            acc = acc * QK_SCALE
        u_ref[:, j * GROUP:(j + 1) * GROUP] = acc.astype(BF16)


def _inproj(x2d, g, w_bf16, tm=512):
    m = x2d.shape[0]
    return pl.pallas_call(
        _inproj_kernel,
        grid=(m // tm,),
        in_specs=[
            pl.BlockSpec((tm, D_MODEL), lambda i: (i, 0)),
            pl.BlockSpec((1, D_MODEL), lambda i: (0, 0)),
            pl.BlockSpec((D_MODEL, IN_WIDTH), lambda i: (0, 0)),
        ],
        out_specs=pl.BlockSpec((tm, IN_WIDTH), lambda i: (i, 0)),
        out_shape=jax.ShapeDtypeStruct((m, IN_WIDTH), BF16),
        compiler_params=pltpu.CompilerParams(
            dimension_semantics=("arbitrary",), vmem_limit_bytes=VMEM_LIMIT),
        name="inproj",
    )(x2d, g, w_bf16)


def _bias_kernel(tab_ref, bkt_ref, out_ref):
    g = pl.program_id(0)
    row = lax.broadcasted_iota(jnp.int32, (BLK, BLK), 0)
    col = lax.broadcasted_iota(jnp.int32, (BLK, BLK), 1)
    for half in range(2):
        head = jnp.where(g < DIFF_HEADS, g, 2 * g - DIFF_HEADS + half)
        for t in range(2):
            bk = bkt_ref[t]
            acc = jnp.zeros((BLK, BLK), F32)
            for b in range(N_BUCKETS):
                acc = jnp.where(bk == b, tab_ref[b, head], acc)
            if t == 0:
                acc = jnp.where(row >= col, acc, -jnp.inf)
            out_ref[0, t, half * BLK:(half + 1) * BLK, :] = acc
        out_ref[0, 2, half * BLK:(half + 1) * BLK, :] = jnp.full((BLK, BLK), tab_ref[N_BUCKETS - 1, head], F32)


def _bias_tiles(rel_bias):
    bkt = jnp.asarray(_bucket_tiles())
    ngroups = DIFF_HEADS + MOBA_HEADS // 2
    return pl.pallas_call(
        _bias_kernel,
        grid=(ngroups,),
        in_specs=[
            pl.BlockSpec(memory_space=pltpu.SMEM),
            pl.BlockSpec((2, BLK, BLK), lambda g: (0, 0, 0)),
        ],
        out_specs=pl.BlockSpec((1, 3, 2 * BLK, BLK), lambda g: (g, 0, 0, 0)),
        out_shape=jax.ShapeDtypeStruct((ngroups, 3, 2 * BLK, BLK), F32),
        compiler_params=pltpu.CompilerParams(dimension_semantics=("arbitrary",)),
        name="bias_tiles",
    )(rel_bias, bkt)


def _slot_plan(pa, k):
    if k <= pa:
        return 0, k, min(pa - k, 2)
    kb = k - pa - 1
    return 1, kb, min(NBLK - 1 - pa - kb, 2)


def _run_pairs(build_qe, k_at, v_at, bias_ref, finish, qe_scr, s_scr, m_scr):
    for ph in range(NPAIR + 1):
        one = jnp.minimum(pl.program_id(0) + 1, 1)

        def body(i, carry, ph=ph):
            _phase(ph, build_qe, k_at, v_at, bias_ref, finish, qe_scr, s_scr, m_scr)
            return carry

        lax.fori_loop(0, one, body, 0)


def _phase(ph, build_qe, k_at, v_at, bias_ref, finish, qe_scr, s_scr, m_scr):
    pa_n = ph if ph < NPAIR else None
    pa_c = ph - 1 if ph >= 1 else None
    par_n, par_c = ph % 2, (ph + 1) % 2
    if pa_n is not None:
        build_qe(pa_n, par_n)
    m_run = [None, None]
    acc_run = [None, None]

    for k in range(NSLOT):
        if pa_n is not None:
            sel, kb, idx = _slot_plan(pa_n, k)
            s = lax.dot_general(qe_scr[par_n, sel], k_at(kb), _NT, preferred_element_type=F32)
            s = s + bias_ref[0, idx]
            s_scr[par_n, k] = s
            t = jnp.broadcast_to(jnp.max(s, axis=-1, keepdims=True), (2 * BLK, LANES))
            m_run[sel] = t if m_run[sel] is None else jnp.maximum(m_run[sel], t)
            if kb == (pa_n, NBLK - 1 - pa_n)[sel]:
                m_scr[par_n, sel] = m_run[sel]
        if pa_c is not None:
            sel, kb, _ = _slot_plan(pa_c, k)
            mb = m_scr[par_c, sel]
            s = s_scr[par_c, k]
            p = jnp.concatenate([jnp.exp(s[:, :LANES] - mb), jnp.exp(s[:, LANES:] - mb)], axis=1).astype(BF16)
            pv = jnp.dot(p, v_at(kb), preferred_element_type=F32)
            acc_run[sel] = pv if acc_run[sel] is None else acc_run[sel] + pv

    if pa_c is not None:
        for t, qt in enumerate((pa_c, NBLK - 1 - pa_c)):
            acc = acc_run[t]
            finish(qt, acc[:, :LANES] / acc[:, LANES:])


def _vext_scratch():
    return pltpu.VMEM((NBLK, BLK, 2 * LANES), BF16)


def _extend_values(v_ref, vext_scr):
    for n in range(NBLK):
        vext_scr[n, :, :LANES] = v_ref[0, n]
        vext_scr[n, :, LANES:] = jnp.ones((BLK, LANES), BF16)


def _silu(g):
    return g / (1.0 + jnp.exp(-g))


def _attention_call(kernel_fn, name, u4, extra_inputs, extra_specs, scratch, col0, ncol, bias0):
    b = u4.shape[0]
    grp = GROUP // LANES

    def seq_spec(offset):
        return pl.BlockSpec((1, NBLK, BLK, LANES), lambda bi, h: (bi, 0, 0, col0 + offset + h))

    return pl.pallas_call(
        kernel_fn,
        grid=(b, ncol),
        in_specs=[seq_spec(0), seq_spec(grp), seq_spec(2 * grp), seq_spec(3 * grp),
                  pl.BlockSpec((1, 3, 2 * BLK, BLK), lambda bi, h: (bias0 + h, 0, 0, 0)),
                  ] + extra_specs,
        out_specs=pl.BlockSpec((NBLK * BLK, LANES), lambda bi, h: (bi, h)),
        out_shape=jax.ShapeDtypeStruct((b * NBLK * BLK, ncol * LANES), BF16),
        scratch_shapes=scratch,
        compiler_params=pltpu.CompilerParams(
            dimension_semantics=("arbitrary", "arbitrary"),
            vmem_limit_bytes=VMEM_LIMIT),
        name=name,
    )(u4, u4, u4, u4, *extra_inputs)


def _core_scratch(kc):
    return [
        pltpu.VMEM((2, 2, 2 * BLK, kc), BF16),
        pltpu.VMEM((2, NSLOT, 2 * BLK, BLK), F32),
        pltpu.VMEM((2, 2, 2 * BLK, LANES), F32),
    ]


def _diff_kernel(q_ref, k_ref, v_ref, g_ref, bias_ref, lam_ref, sg_ref, o_ref,
                 vext_scr, qe_scr, s_scr, m_scr):
    _extend_values(v_ref, vext_scr)
    lane = lax.broadcasted_iota(jnp.int32, (BLK, LANES), 1)
    lp = lam_ref[...]
    lambda_init = 0.8 - 0.6 * math.exp(-0.3 * 0)
    lam = (jnp.exp(jnp.sum(lp[0:1] * lp[1:2], axis=-1, keepdims=True))
           - jnp.exp(jnp.sum(lp[2:3] * lp[3:4], axis=-1, keepdims=True)) + lambda_init)

    def build_qe(pa, par):
        for t, qt in enumerate((pa, NBLK - 1 - pa)):
            q = q_ref[0, qt].astype(F32)
            qe_scr[par, t, :BLK, :] = jnp.where(lane < DIFF_QK, q, 0.0).astype(BF16)
            qe_scr[par, t, BLK:, :] = jnp.where(lane >= DIFF_QK, q, 0.0).astype(BF16)

    def finish(qt, o):
        out = o[:BLK] - lam * o[BLK:]
        ms = jnp.mean(out * out, axis=-1, keepdims=True)
        out = out * lax.rsqrt(ms + SUBLN_EPS) * sg_ref[...] * (1.0 - lambda_init)
        out = out * _silu(g_ref[0, qt].astype(F32))
        o_ref[qt * BLK:(qt + 1) * BLK, :] = out.astype(BF16)

    _run_pairs(build_qe, lambda kb: k_ref[0, kb], lambda kb: vext_scr[kb], bias_ref, finish,
               qe_scr, s_scr, m_scr)


def _diff_attention(u4, bias, diff_lambda, subln_g):
    extra_specs = [pl.BlockSpec((4, DIFF_QK), lambda bi, h: (0, 0)),
                   pl.BlockSpec((1, LANES), lambda bi, h: (0, 0))]
    return _attention_call(_diff_kernel, "diff_attention", u4, (bias, diff_lambda, subln_g), extra_specs,
                           [_vext_scratch()] + _core_scratch(LANES), col0=0, ncol=DIFF_HEADS, bias0=0)


def _moba_kernel(q_ref, k_ref, v_ref, g_ref, bias_ref, o_ref,
                 kext_scr, wg_scr, vext_scr, qe_scr, s_scr, m_scr):
    _extend_values(v_ref, vext_scr)
    lane = lax.broadcasted_iota(jnp.int32, (BLK, LANES), 1)
    wrow = lax.broadcasted_iota(jnp.int32, (LANES, LANES), 0)
    wlane = lax.broadcasted_iota(jnp.int32, (LANES, LANES), 1)
    wg = jnp.zeros((LANES, LANES), F32)
    for n in range(NBLK):
        kblk = k_ref[0, n]
        kext_scr[n, :, :LANES] = kblk
        onehot = jnp.where((lane == n) | (lane == NBLK + n), 1.0, 0.0)
        kext_scr[n, :, LANES:] = onehot.astype(BF16)
        kmean = jnp.sum(kblk.astype(F32), axis=0, keepdims=True) * (1.0 / BLK)
        hi = kmean.astype(BF16).astype(F32)
        mid = (kmean - hi).astype(BF16).astype(F32)
        lo = (kmean - hi - mid).astype(BF16).astype(F32)
        for piece, val in enumerate((hi, mid, lo)):
            for hh in range(2):
                r = piece * 2 * NBLK + hh * NBLK + n
                head_lanes = (wlane >= hh * MOBA_DIM) & (wlane < (hh + 1) * MOBA_DIM)
                wg = jnp.where((wrow == r) & head_lanes, val, wg)
    wg_scr[...] = wg.astype(BF16)

    n_iota = lax.broadcasted_iota(jnp.int32, (NBLK, BLK), 0)

    def build_qe(pa, par):
        for t, qt in enumerate((pa, NBLK - 1 - pa)):
            q = q_ref[0, qt]
            qf = q.astype(F32)
            if qt == 0:
                for hh in range(2):
                    head_lanes = (lane >= hh * MOBA_DIM) & (lane < (hh + 1) * MOBA_DIM)
                    qe_scr[par, t, hh * BLK:(hh + 1) * BLK, :LANES] = jnp.where(head_lanes, qf, 0.0).astype(BF16)
                    qe_scr[par, t, hh * BLK:(hh + 1) * BLK, LANES:] = jnp.where(lane < 2 * NBLK, 0.0, qf).astype(BF16)
                continue
            g_t = lax.dot_general(wg_scr[...], q, _NT, preferred_element_type=F32)
            g_t = g_t[0:2 * NBLK] + g_t[2 * NBLK:4 * NBLK] + g_t[4 * NBLK:6 * NBLK]
            mask_rows = []
            for hh in range(2):
                g8 = g_t[hh * NBLK:(hh + 1) * NBLK]
                cnt = jnp.zeros((NBLK, BLK), jnp.int32)
                for i in range(qt):
                    gi = g8[i:i + 1]
                    beats = (gi > g8) | ((gi == g8) & (i < n_iota))
                    cnt = cnt + jnp.where(beats, 1, 0)
                keep = ((n_iota < qt) & (cnt < MOBA_TOPK)) | (n_iota == qt)
                mask_rows.append(jnp.where(keep, 0.0, -MASK_BIG))
            mask_t = jnp.concatenate(mask_rows + [jnp.zeros((LANES - 2 * NBLK, BLK), F32)], axis=0)
            mask = mask_t.T
            for hh in range(2):
                head_lanes = (lane >= hh * MOBA_DIM) & (lane < (hh + 1) * MOBA_DIM)
                mask_lanes = (lane >= hh * NBLK) & (lane < (hh + 1) * NBLK)
                qe_scr[par, t, hh * BLK:(hh + 1) * BLK, :LANES] = jnp.where(head_lanes, qf, 0.0).astype(BF16)
                qe_scr[par, t, hh * BLK:(hh + 1) * BLK, LANES:] = jnp.where(mask_lanes, mask, 0.0).astype(BF16)

    def finish(qt, o):
        out = jnp.where(lane < MOBA_DIM, o[:BLK], o[BLK:])
        out = out * _silu(g_ref[0, qt].astype(F32))
        o_ref[qt * BLK:(qt + 1) * BLK, :] = out.astype(BF16)

    _run_pairs(build_qe, lambda kb: kext_scr[kb], lambda kb: vext_scr[kb], bias_ref, finish,
               qe_scr, s_scr, m_scr)


def _moba_attention(u4, bias):
    scratch = [pltpu.VMEM((NBLK, BLK, 2 * LANES), BF16),
               pltpu.VMEM((LANES, LANES), BF16),
               _vext_scratch()] + _core_scratch(2 * LANES)
    return _attention_call(_moba_kernel, "moba_attention", u4, (bias,), [], scratch,
                           col0=4 * (GROUP // LANES), ncol=MOBA_HEADS // 2, bias0=DIFF_HEADS)


def _outproj_kernel(a_ref, b_ref, x_ref, g_ref, w_ref, o_ref):
    half = a_ref.shape[1]
    y = jnp.dot(a_ref[...], w_ref[:half, :], preferred_element_type=F32)
    y = y + jnp.dot(b_ref[...], w_ref[half:, :], preferred_element_type=F32)
    ms = jnp.mean(y * y, axis=-1, keepdims=True)
    o_ref[...] = x_ref[...] + y * lax.rsqrt(ms + NORM_EPS) * g_ref[...]


def _outproj(a, b, x2d, g, w_bf16, tm=512):
    m = x2d.shape[0]
    half = a.shape[1]
    return pl.pallas_call(
        _outproj_kernel,
        grid=(m // tm,),
        in_specs=[
            pl.BlockSpec((tm, half), lambda i: (i, 0)),
            pl.BlockSpec((tm, half), lambda i: (i, 0)),
            pl.BlockSpec((tm, D_MODEL), lambda i: (i, 0)),
            pl.BlockSpec((1, D_MODEL), lambda i: (0, 0)),
            pl.BlockSpec((2 * half, D_MODEL), lambda i: (0, 0)),
        ],
        out_specs=pl.BlockSpec((tm, D_MODEL), lambda i: (i, 0)),
        out_shape=jax.ShapeDtypeStruct((m, D_MODEL), F32),
        compiler_params=pltpu.CompilerParams(
            dimension_semantics=("arbitrary",), vmem_limit_bytes=VMEM_LIMIT),
        name="outproj",
    )(a, b, x2d, g, w_bf16)


def kernel(x, norm_pre_g, w_in, diff_lambda, diff_subln_g, w_out, norm_post_g, rel_bias):
    b, s, d = x.shape
    assert d == D_MODEL and s == NBLK * BLK
    assert norm_pre_g.shape[0] == 1, "single-layer block"
    x2d = x.reshape(b * s, d)
    u = _inproj(x2d, norm_pre_g[0:1], w_in[0].astype(BF16))
    u4 = u.reshape(b, NBLK, BLK, IN_WIDTH)
    bias = _bias_tiles(rel_bias)
    a_out = _diff_attention(u4, bias, diff_lambda[0], diff_subln_g[0:1])
    b_out = _moba_attention(u4, bias)
    out = _outproj(a_out, b_out, x2d, norm_post_g[0:1], w_out[0].astype(BF16))
    return out.reshape(b, s, d)
```

```python
import math

import numpy as np
import jax
import jax.numpy as jnp
from jax import lax
from jax.experimental import pallas as pl
from jax.experimental.pallas import tpu as pltpu

F32 = jnp.float32
BF16 = jnp.bfloat16

D_MODEL = 1024
DIFF_HEADS = 4
DIFF_QK = 64
MOBA_HEADS = 8
MOBA_DIM = 64
MOBA_TOPK = 3
N_BUCKETS = 32
MAX_DISTANCE = 128
IN_WIDTH = 4096
GROUP = 512
BLK = 256
NBLK = 8
NSLOT = NBLK + 1
PAIR_GROUPS = ((0, 1), (2, 3))
GROUP_PAIRS = 2
LANES = 128
NORM_EPS = 1e-6
SUBLN_EPS = 1e-5
QK_SCALE = 0.125
LOG2E = math.log2(math.e)
MASK_BIG = 1e30
VMEM_LIMIT = 56 * 1024 * 1024

QKG_WIDTH = 5 * GROUP
KEXT_WIDTH = 2 * GROUP
VEXT_WIDTH = 4 * GROUP

_NT = (((1,), (1,)), ((), ()))


def _bucket_tiles():
    i = np.arange(BLK)[:, None]
    j = np.arange(BLK)[None, :]
    out = []
    for base in (0, BLK):
        n = np.maximum(base + i - j, 0)
        max_exact = N_BUCKETS // 2
        nf = np.maximum(n, 1).astype(np.float32)
        ratio = np.log(nf / np.float32(max_exact)) / np.float32(math.log(MAX_DISTANCE / max_exact))
        large = max_exact + (ratio * np.float32(N_BUCKETS - max_exact)).astype(np.int32)
        large = np.minimum(large, N_BUCKETS - 1)
        out.append(np.where(n < max_exact, n, large).astype(np.int32))
    return np.stack(out)


def _inproj_kernel(x_ref, g_ref, w_ref, qkg_ref, kext_ref, vext_ref, kmean_ref):
    tm = x_ref.shape[0]
    x = x_ref[...]
    ms = jnp.mean(x * x, axis=-1, keepdims=True)
    h = (x * lax.rsqrt(ms + NORM_EPS) * g_ref[...]).astype(BF16)

    def project(j):
        return jnp.dot(h, w_ref[:, j * GROUP:(j + 1) * GROUP], preferred_element_type=F32)

    qkg_ref[:, 0 * GROUP:1 * GROUP] = (project(0) * (QK_SCALE * LOG2E)).astype(BF16)
    qkg_ref[:, 1 * GROUP:2 * GROUP] = project(1).astype(BF16)
    qkg_ref[:, 2 * GROUP:3 * GROUP] = project(3).astype(BF16)
    qkg_ref[:, 3 * GROUP:4 * GROUP] = (project(4) * (QK_SCALE * LOG2E)).astype(BF16)
    qkg_ref[:, 4 * GROUP:5 * GROUP] = project(7).astype(BF16)

    ones = jnp.ones((tm, LANES), BF16)
    for base, j in ((0, 2), (GROUP // LANES, 6)):
        v = project(j).astype(BF16)
        for c in range(GROUP // LANES):
            vext_ref[:, (base + c) * 2 * LANES:(base + c) * 2 * LANES + LANES] = v[:, c * LANES:(c + 1) * LANES]
            vext_ref[:, (base + c) * 2 * LANES + LANES:(base + c + 1) * 2 * LANES] = ones

    mk = project(5)
    lane = lax.broadcasted_iota(jnp.int32, (BLK, LANES), 1)
    blocks_per_tile = tm // BLK
    first_block = (pl.program_id(0) * blocks_per_tile) % NBLK
    for r in range(blocks_per_tile):
        blk = mk[r * BLK:(r + 1) * BLK]
        kmean_ref[0, r:r + 1, :] = jnp.sum(blk, axis=0, keepdims=True) * (1.0 / BLK)
        n = first_block + r
        onehot = jnp.where((lane == n) | (lane == NBLK + n), 1.0, 0.0).astype(BF16)
        for p in range(GROUP // LANES):
            kext_ref[r * BLK:(r + 1) * BLK, p * 2 * LANES:p * 2 * LANES + LANES] = (
                blk[:, p * LANES:(p + 1) * LANES].astype(BF16))
            kext_ref[r * BLK:(r + 1) * BLK, p * 2 * LANES + LANES:(p + 1) * 2 * LANES] = onehot


def _inproj(x2d, g, w_bf16, tm=2 * BLK):
    m = x2d.shape[0]
    row = lambda i: (i, 0)
    return pl.pallas_call(
        _inproj_kernel,
        grid=(m // tm,),
        in_specs=[
            pl.BlockSpec((tm, D_MODEL), row),
            pl.BlockSpec((1, D_MODEL), lambda i: (0, 0)),
            pl.BlockSpec((D_MODEL, IN_WIDTH), lambda i: (0, 0)),
        ],
        out_specs=[
            pl.BlockSpec((tm, QKG_WIDTH), row),
            pl.BlockSpec((tm, KEXT_WIDTH), row),
            pl.BlockSpec((tm, VEXT_WIDTH), row),
            pl.BlockSpec((1, tm // BLK, GROUP), lambda i: (i, 0, 0)),
        ],
        out_shape=[
            jax.ShapeDtypeStruct((m, QKG_WIDTH), BF16),
            jax.ShapeDtypeStruct((m, KEXT_WIDTH), BF16),
            jax.ShapeDtypeStruct((m, VEXT_WIDTH), BF16),
            jax.ShapeDtypeStruct((m // tm, tm // BLK, GROUP), F32),
        ],
        compiler_params=pltpu.CompilerParams(
            dimension_semantics=("arbitrary",), vmem_limit_bytes=VMEM_LIMIT),
        name="inproj",
    )(x2d, g, w_bf16)


def _bias_kernel(tab_ref, bkt_ref, out_ref):
    g = pl.program_id(0)
    row = lax.broadcasted_iota(jnp.int32, (BLK, BLK), 0)
    col = lax.broadcasted_iota(jnp.int32, (BLK, BLK), 1)
    for half in range(2):
        head = jnp.where(g < DIFF_HEADS, g, 2 * g - DIFF_HEADS + half)
        for t in range(2):
            bk = bkt_ref[t]
            acc = jnp.zeros((BLK, BLK), F32)
            for b in range(N_BUCKETS):
                acc = jnp.where(bk == b, tab_ref[b, head] * LOG2E, acc)
            if t == 0:
                acc = jnp.where(row >= col, acc, -jnp.inf)
            out_ref[0, t, half * BLK:(half + 1) * BLK, :] = acc
        out_ref[0, 2, half * BLK:(half + 1) * BLK, :] = jnp.full((BLK, BLK), tab_ref[N_BUCKETS - 1, head] * LOG2E, F32)


def _bias_tiles(rel_bias):
    bkt = jnp.asarray(_bucket_tiles())
    ngroups = DIFF_HEADS + MOBA_HEADS // 2
    return pl.pallas_call(
        _bias_kernel,
        grid=(ngroups,),
        in_specs=[
            pl.BlockSpec(memory_space=pltpu.SMEM),
            pl.BlockSpec((2, BLK, BLK), lambda g: (0, 0, 0)),
        ],
        out_specs=pl.BlockSpec((1, 3, 2 * BLK, BLK), lambda g: (g, 0, 0, 0)),
        out_shape=jax.ShapeDtypeStruct((ngroups, 3, 2 * BLK, BLK), F32),
        compiler_params=pltpu.CompilerParams(dimension_semantics=("arbitrary",)),
        name="bias_tiles",
    )(rel_bias, bkt)


def _slot_plan(pa, k):
    if k <= pa:
        return 0, k, min(pa - k, 2)
    kb = k - pa - 1
    return 1, kb, min(NBLK - 1 - pa - kb, 2)


def _emit_streams(scr, scores=None, weighted=None):
    qe_scr, s_scr, m_scr, acc_scr = scr
    for j in range(GROUP_PAIRS):
        m_run = [None, None]
        acc_run = [None, None]
        for k in range(NSLOT):
            if scores is not None:
                g, k_at, bias_ref = scores
                pa = PAIR_GROUPS[g][j]
                sel, kb, idx = _slot_plan(pa, k)
                s = lax.dot_general(qe_scr[g, j, sel], k_at(kb), _NT, preferred_element_type=F32)
                s = s + bias_ref[0, idx]
                s_scr[g, j, k] = s
                t = jnp.maximum(s[:, :LANES], s[:, LANES:])
                m_run[sel] = t if m_run[sel] is None else jnp.maximum(m_run[sel], t)
                if kb == (pa, NBLK - 1 - pa)[sel]:
                    m_row = jnp.max(m_run[sel], axis=-1, keepdims=True)
                    m_scr[g, j, sel] = jnp.broadcast_to(m_row, (2 * BLK, LANES))
            if weighted is not None:
                g, v_at = weighted
                pa = PAIR_GROUPS[g][j]
                sel, kb, _ = _slot_plan(pa, k)
                mb = m_scr[g, j, sel]
                s = s_scr[g, j, k]
                p = jnp.concatenate([jnp.exp2(s[:, :LANES] - mb), jnp.exp2(s[:, LANES:] - mb)], axis=1).astype(BF16)
                pv = jnp.dot(p, v_at(kb), preferred_element_type=F32)
                acc_run[sel] = pv if acc_run[sel] is None else acc_run[sel] + pv
                if kb == (pa, NBLK - 1 - pa)[sel]:
                    acc_scr[g, j, sel] = acc_run[sel]


def _emit_epilogue(g, acc_scr, finish):
    for j, pa in enumerate(PAIR_GROUPS[g]):
        for t, qt in enumerate((pa, NBLK - 1 - pa)):
            acc = acc_scr[g, j, t]
            finish(qt, acc[:, :LANES] / acc[:, LANES:])


def _region(trips, fn):
    def body(i, carry):
        fn()
        return carry
    lax.fori_loop(0, trips, body, 0)


def _pipelined_step(build_cur, build_nxt, k_cur, k_nxt, v_cur, bias_cur, bias_nxt, finish, scr):
    acc_scr = scr[3]
    first = (pl.program_id(0) == 0) & (pl.program_id(1) == 0)
    one = jnp.minimum(pl.program_id(0) + 1, 1)

    def prologue():
        build_cur(0)
        build_cur(1)

    def prologue_scores():
        _emit_streams(scr, scores=(0, k_cur, bias_cur))

    def region1():
        build_nxt(0)
        _emit_streams(scr, scores=(1, k_cur, bias_cur), weighted=(0, v_cur))

    def region2():
        build_nxt(1)
        _emit_streams(scr, scores=(0, k_nxt, bias_nxt), weighted=(1, v_cur))
        _emit_epilogue(0, acc_scr, finish)

    _region(jnp.where(first, 1, 0), prologue)
    _region(jnp.where(first, 1, 0), prologue_scores)
    _region(one, region1)
    _region(one, region2)
    _region(one, lambda: _emit_epilogue(1, acc_scr, finish))


def _silu(g):
    return g / (1.0 + jnp.exp(-g))


def _core_scratch(kc):
    ng = len(PAIR_GROUPS)
    return [
        pltpu.VMEM((ng, GROUP_PAIRS, 2, 2 * BLK, kc), BF16),
        pltpu.VMEM((ng, GROUP_PAIRS, NSLOT, 2 * BLK, BLK), F32),
        pltpu.VMEM((ng, GROUP_PAIRS, 2, 2 * BLK, LANES), F32),
        pltpu.VMEM((ng, GROUP_PAIRS, 2, 2 * BLK, 2 * LANES), F32),
    ]


def _step_maps(nb, ncol):
    def cur(bi, h):
        return bi, h

    def nxt(bi, h):
        flat = jnp.minimum(bi * ncol + h + 1, nb * ncol - 1)
        return flat // ncol, flat % ncol

    return cur, nxt


def _seq_spec(width, which, col0):
    def index_map(bi, h):
        b2, h2 = which(bi, h)
        return b2, 0, 0, col0 + h2
    return pl.BlockSpec((1, NBLK, BLK, width), index_map)


def _bias_spec(which, bias0):
    def index_map(bi, h):
        _, h2 = which(bi, h)
        return bias0 + h2, 0, 0, 0
    return pl.BlockSpec((1, 3, 2 * BLK, BLK), index_map)


def _diff_kernel(q_cur, q_nxt, k_cur, k_nxt, v_cur, g_cur, bias_cur, bias_nxt, lam_ref, sg_ref, o_ref, *scr):
    qe_scr = scr[0]
    lane = lax.broadcasted_iota(jnp.int32, (BLK, LANES), 1)

    def build(q_ref, g):
        for j, pa in enumerate(PAIR_GROUPS[g]):
            for t, qt in enumerate((pa, NBLK - 1 - pa)):
                q = q_ref[0, qt].astype(F32)
                qe_scr[g, j, t, :BLK, :] = jnp.where(lane < DIFF_QK, q, 0.0).astype(BF16)
                qe_scr[g, j, t, BLK:, :] = jnp.where(lane >= DIFF_QK, q, 0.0).astype(BF16)

    def finish(qt, o):
        lp = lam_ref[...]
        lambda_init = 0.8 - 0.6 * math.exp(-0.3 * 0)
        lam = (jnp.exp(jnp.sum(lp[0:1] * lp[1:2], axis=-1, keepdims=True))
               - jnp.exp(jnp.sum(lp[2:3] * lp[3:4], axis=-1, keepdims=True)) + lambda_init)
        out = o[:BLK] - lam * o[BLK:]
        ms = jnp.mean(out * out, axis=-1, keepdims=True)
        out = out * lax.rsqrt(ms + SUBLN_EPS) * sg_ref[...] * (1.0 - lambda_init)
        out = out * _silu(g_cur[0, qt].astype(F32))
        o_ref[qt * BLK:(qt + 1) * BLK, :] = out.astype(BF16)

    _pipelined_step(lambda g: build(q_cur, g), lambda g: build(q_nxt, g),
                    lambda kb: k_cur[0, kb], lambda kb: k_nxt[0, kb], lambda kb: v_cur[0, kb],
                    bias_cur, bias_nxt, finish, scr)


def _diff_attention(qkg4, vext4, bias, diff_lambda, subln_g):
    nb = qkg4.shape[0]
    grp = GROUP // LANES
    cur, nxt = _step_maps(nb, DIFF_HEADS)
    const = lambda bi, h: (0, 0)
    return pl.pallas_call(
        _diff_kernel,
        grid=(nb, DIFF_HEADS),
        in_specs=[_seq_spec(LANES, cur, 0), _seq_spec(LANES, nxt, 0),
                  _seq_spec(LANES, cur, grp), _seq_spec(LANES, nxt, grp),
                  _seq_spec(2 * LANES, cur, 0), _seq_spec(LANES, cur, 2 * grp),
                  _bias_spec(cur, 0), _bias_spec(nxt, 0),
                  pl.BlockSpec((4, DIFF_QK), const), pl.BlockSpec((1, LANES), const)],
        out_specs=pl.BlockSpec((NBLK * BLK, LANES), lambda bi, h: (bi, h)),
        out_shape=jax.ShapeDtypeStruct((nb * NBLK * BLK, DIFF_HEADS * LANES), BF16),
        scratch_shapes=_core_scratch(LANES),
        compiler_params=pltpu.CompilerParams(
            dimension_semantics=("arbitrary", "arbitrary"), vmem_limit_bytes=VMEM_LIMIT),
        name="diff_attention",
    )(qkg4, qkg4, qkg4, qkg4, vext4, qkg4, bias, bias, diff_lambda, subln_g)


def _moba_kernel(q_cur, q_nxt, k_cur, k_nxt, v_cur, g_cur, bias_cur, bias_nxt, kmean_cur, kmean_nxt, o_ref,
                 wg_scr, *scr):
    qe_scr = scr[0]
    lane = lax.broadcasted_iota(jnp.int32, (BLK, LANES), 1)
    n_iota = lax.broadcasted_iota(jnp.int32, (NBLK, BLK), 0)

    def gate_operand(kmean_ref):
        wrow = lax.broadcasted_iota(jnp.int32, (LANES, LANES), 0)
        wlane = lax.broadcasted_iota(jnp.int32, (LANES, LANES), 1)
        wg = jnp.zeros((LANES, LANES), F32)
        per_tile = kmean_ref.shape[1]
        for n in range(NBLK):
            kmean = kmean_ref[n // per_tile, n % per_tile:n % per_tile + 1, :]
            hi = kmean.astype(BF16).astype(F32)
            mid = (kmean - hi).astype(BF16).astype(F32)
            lo = (kmean - hi - mid).astype(BF16).astype(F32)
            for piece, val in enumerate((hi, mid, lo)):
                for hh in range(2):
                    r = piece * 2 * NBLK + hh * NBLK + n
                    head_lanes = (wlane >= hh * MOBA_DIM) & (wlane < (hh + 1) * MOBA_DIM)
                    wg = jnp.where((wrow == r) & head_lanes, val, wg)
        return wg.astype(BF16)

    def build(q_ref, wg, g):
        for j, pa in enumerate(PAIR_GROUPS[g]):
            for t, qt in enumerate((pa, NBLK - 1 - pa)):
                q = q_ref[0, qt]
                qf = q.astype(F32)
                if qt == 0:
                    mask = jnp.zeros((BLK, LANES), F32)
                else:
                    g_t = lax.dot_general(wg, q, _NT, preferred_element_type=F32)
                    g_t = g_t[0:2 * NBLK] + g_t[2 * NBLK:4 * NBLK] + g_t[4 * NBLK:6 * NBLK]
                    mask_rows = []
                    for hh in range(2):
                        g8 = g_t[hh * NBLK:(hh + 1) * NBLK]
                        cnt = jnp.zeros((NBLK, BLK), jnp.int32)
                        for i in range(qt):
                            gi = g8[i:i + 1]
                            beats = (gi > g8) | ((gi == g8) & (i < n_iota))
                            cnt = cnt + jnp.where(beats, 1, 0)
                        keep = ((n_iota < qt) & (cnt < MOBA_TOPK)) | (n_iota == qt)
                        mask_rows.append(jnp.where(keep, 0.0, -MASK_BIG))
                    mask_t = jnp.concatenate(mask_rows + [jnp.zeros((LANES - 2 * NBLK, BLK), F32)], axis=0)
                    mask = mask_t.T
                for hh in range(2):
                    head_lanes = (lane >= hh * MOBA_DIM) & (lane < (hh + 1) * MOBA_DIM)
                    mask_lanes = (lane >= hh * NBLK) & (lane < (hh + 1) * NBLK)
                    qe_scr[g, j, t, hh * BLK:(hh + 1) * BLK, :LANES] = jnp.where(head_lanes, qf, 0.0).astype(BF16)
                    qe_scr[g, j, t, hh * BLK:(hh + 1) * BLK, LANES:] = jnp.where(mask_lanes, mask, 0.0).astype(BF16)

    def build_cur(g):
        build(q_cur, gate_operand(kmean_cur), g)

    def build_nxt(g):
        if g == 0:
            wg = gate_operand(kmean_nxt)
            wg_scr[...] = wg
        else:
            wg = wg_scr[...]
        build(q_nxt, wg, g)

    def finish(qt, o):
        out = jnp.where(lane < MOBA_DIM, o[:BLK], o[BLK:])
        out = out * _silu(g_cur[0, qt].astype(F32))
        o_ref[qt * BLK:(qt + 1) * BLK, :] = out.astype(BF16)

    _pipelined_step(build_cur, build_nxt,
                    lambda kb: k_cur[0, kb], lambda kb: k_nxt[0, kb], lambda kb: v_cur[0, kb],
                    bias_cur, bias_nxt, finish, scr)


def _moba_attention(qkg4, kext4, vext4, kmean, bias):
    nb = qkg4.shape[0]
    grp = GROUP // LANES
    pairs = MOBA_HEADS // 2
    cur, nxt = _step_maps(nb, pairs)
    tiles = kmean.shape[0] // nb

    def kmean_spec(which):
        def index_map(bi, h):
            b2, h2 = which(bi, h)
            return b2, 0, h2
        return pl.BlockSpec((tiles, kmean.shape[1], LANES), index_map)

    return pl.pallas_call(
        _moba_kernel,
        grid=(nb, pairs),
        in_specs=[_seq_spec(LANES, cur, 3 * grp), _seq_spec(LANES, nxt, 3 * grp),
                  _seq_spec(2 * LANES, cur, 0), _seq_spec(2 * LANES, nxt, 0),
                  _seq_spec(2 * LANES, cur, grp), _seq_spec(LANES, cur, 4 * grp),
                  _bias_spec(cur, DIFF_HEADS), _bias_spec(nxt, DIFF_HEADS),
                  kmean_spec(cur), kmean_spec(nxt)],
        out_specs=pl.BlockSpec((NBLK * BLK, LANES), lambda bi, h: (bi, h)),
        out_shape=jax.ShapeDtypeStruct((nb * NBLK * BLK, pairs * LANES), BF16),
        scratch_shapes=[pltpu.VMEM((LANES, LANES), BF16)] + _core_scratch(2 * LANES),
        compiler_params=pltpu.CompilerParams(
            dimension_semantics=("arbitrary", "arbitrary"), vmem_limit_bytes=VMEM_LIMIT),
        name="moba_attention",
    )(qkg4, qkg4, kext4, kext4, vext4, qkg4, bias, bias, kmean, kmean)


def _outproj_kernel(a_ref, b_ref, x_ref, g_ref, w_ref, o_ref):
    half = a_ref.shape[1]
    y = jnp.dot(a_ref[...], w_ref[:half, :], preferred_element_type=F32)
    y = y + jnp.dot(b_ref[...], w_ref[half:, :], preferred_element_type=F32)
    ms = jnp.mean(y * y, axis=-1, keepdims=True)
    o_ref[...] = x_ref[...] + y * lax.rsqrt(ms + NORM_EPS) * g_ref[...]


def _outproj(a, b, x2d, g, w_bf16, tm=512):
    m = x2d.shape[0]
    half = a.shape[1]
    return pl.pallas_call(
        _outproj_kernel,
        grid=(m // tm,),
        in_specs=[
            pl.BlockSpec((tm, half), lambda i: (i, 0)),
            pl.BlockSpec((tm, half), lambda i: (i, 0)),
            pl.BlockSpec((tm, D_MODEL), lambda i: (i, 0)),
            pl.BlockSpec((1, D_MODEL), lambda i: (0, 0)),
            pl.BlockSpec((2 * half, D_MODEL), lambda i: (0, 0)),
        ],
        out_specs=pl.BlockSpec((tm, D_MODEL), lambda i: (i, 0)),
        out_shape=jax.ShapeDtypeStruct((m, D_MODEL), F32),
        compiler_params=pltpu.CompilerParams(
            dimension_semantics=("arbitrary",), vmem_limit_bytes=VMEM_LIMIT),
        name="outproj",
    )(a, b, x2d, g, w_bf16)


def kernel(x, norm_pre_g, w_in, diff_lambda, diff_subln_g, w_out, norm_post_g, rel_bias):
    b, s, d = x.shape
    assert d == D_MODEL and s == NBLK * BLK
    assert norm_pre_g.shape[0] == 1, "single-layer block"
    x2d = x.reshape(b * s, d)
    qkg, kext, vext, kmean = _inproj(x2d, norm_pre_g[0:1], w_in[0].astype(BF16))
    qkg4 = qkg.reshape(b, NBLK, BLK, QKG_WIDTH)
    kext4 = kext.reshape(b, NBLK, BLK, KEXT_WIDTH)
    vext4 = vext.reshape(b, NBLK, BLK, VEXT_WIDTH)
    bias = _bias_tiles(rel_bias)
    a_out = _diff_attention(qkg4, vext4, bias, diff_lambda[0], diff_subln_g[0:1])
    b_out = _moba_attention(qkg4, kext4, vext4, kmean, bias)
    out = _outproj(a_out, b_out, x2d, norm_post_g[0:1], w_out[0].astype(BF16))
    return out.reshape(b, s, d)
```

```python
import math

import numpy as np
import jax
import jax.numpy as jnp
from jax import lax
from jax.experimental import pallas as pl
from jax.experimental.pallas import tpu as pltpu

F32 = jnp.float32
BF16 = jnp.bfloat16

D_MODEL = 1024
DIFF_HEADS = 4
DIFF_QK = 64
MOBA_HEADS = 8
MOBA_DIM = 64
MOBA_TOPK = 3
N_BUCKETS = 32
MAX_DISTANCE = 128
IN_WIDTH = 4096
GROUP = 512
BLK = 256
NBLK = 8
NSLOT = NBLK + 1
PAIR_GROUPS = ((0, 1), (2, 3))
GROUP_PAIRS = 2
LANES = 128
NORM_EPS = 1e-6
SUBLN_EPS = 1e-5
QK_SCALE = 0.125
LOG2E = math.log2(math.e)
MASK_BIG = 1e30
VMEM_LIMIT = 56 * 1024 * 1024

QKG_WIDTH = 5 * GROUP
KEXT_WIDTH = 2 * GROUP
VEXT_WIDTH = 4 * GROUP

_NT = (((1,), (1,)), ((), ()))


def _rel_bucket(n):
    max_exact = N_BUCKETS // 2
    nf = np.maximum(n, 1).astype(np.float32)
    ratio = np.log(nf / np.float32(max_exact)) / np.float32(math.log(MAX_DISTANCE / max_exact))
    large = max_exact + (ratio * np.float32(N_BUCKETS - max_exact)).astype(np.int32)
    large = np.minimum(large, N_BUCKETS - 1)
    return np.where(n < max_exact, n, large).astype(np.int32)


def _bucket_vectors():
    l = np.arange(2 * BLK)
    out = []
    for t in range(2):
        d = t * BLK + BLK - 1 - l
        out.append(np.where(d >= 0, _rel_bucket(np.maximum(d, 0)), -1).astype(np.int32))
    return np.stack(out)[:, None, :]


def _inproj_kernel(x_ref, g_ref, w_ref, qkg_ref, kext_ref, vext_ref, kmean_ref):
    tm = x_ref.shape[0]
    x = x_ref[...]
    ms = jnp.mean(x * x, axis=-1, keepdims=True)
    h = (x * lax.rsqrt(ms + NORM_EPS) * g_ref[...]).astype(BF16)

    def project(j):
        return jnp.dot(h, w_ref[:, j * GROUP:(j + 1) * GROUP], preferred_element_type=F32)

    qkg_ref[:, 0 * GROUP:1 * GROUP] = (project(0) * (QK_SCALE * LOG2E)).astype(BF16)
    qkg_ref[:, 1 * GROUP:2 * GROUP] = project(1).astype(BF16)
    qkg_ref[:, 2 * GROUP:3 * GROUP] = project(3).astype(BF16)
    qkg_ref[:, 3 * GROUP:4 * GROUP] = (project(4) * (QK_SCALE * LOG2E)).astype(BF16)
    qkg_ref[:, 4 * GROUP:5 * GROUP] = project(7).astype(BF16)

    ones = jnp.ones((tm, LANES), BF16)
    for base, j in ((0, 2), (GROUP // LANES, 6)):
        v = project(j).astype(BF16)
        for c in range(GROUP // LANES):
            vext_ref[:, (base + c) * 2 * LANES:(base + c) * 2 * LANES + LANES] = v[:, c * LANES:(c + 1) * LANES]
            vext_ref[:, (base + c) * 2 * LANES + LANES:(base + c + 1) * 2 * LANES] = ones

    mk = project(5)
    lane = lax.broadcasted_iota(jnp.int32, (BLK, LANES), 1)
    blocks_per_tile = tm // BLK
    first_block = (pl.program_id(0) * blocks_per_tile) % NBLK
    for r in range(blocks_per_tile):
        blk = mk[r * BLK:(r + 1) * BLK]
        kmean_ref[0, r:r + 1, :] = jnp.sum(blk, axis=0, keepdims=True) * (1.0 / BLK)
        n = first_block + r
        onehot = jnp.where((lane == n) | (lane == NBLK + n), 1.0, 0.0).astype(BF16)
        for p in range(GROUP // LANES):
            kext_ref[r * BLK:(r + 1) * BLK, p * 2 * LANES:p * 2 * LANES + LANES] = (
                blk[:, p * LANES:(p + 1) * LANES].astype(BF16))
            kext_ref[r * BLK:(r + 1) * BLK, p * 2 * LANES + LANES:(p + 1) * 2 * LANES] = onehot


def _inproj(x2d, g, w_bf16, tm=2 * BLK):
    m = x2d.shape[0]
    row = lambda i: (i, 0)
    return pl.pallas_call(
        _inproj_kernel,
        grid=(m // tm,),
        in_specs=[
            pl.BlockSpec((tm, D_MODEL), row),
            pl.BlockSpec((1, D_MODEL), lambda i: (0, 0)),
            pl.BlockSpec((D_MODEL, IN_WIDTH), lambda i: (0, 0)),
        ],
        out_specs=[
            pl.BlockSpec((tm, QKG_WIDTH), row),
            pl.BlockSpec((tm, KEXT_WIDTH), row),
            pl.BlockSpec((tm, VEXT_WIDTH), row),
            pl.BlockSpec((1, tm // BLK, GROUP), lambda i: (i, 0, 0)),
        ],
        out_shape=[
            jax.ShapeDtypeStruct((m, QKG_WIDTH), BF16),
            jax.ShapeDtypeStruct((m, KEXT_WIDTH), BF16),
            jax.ShapeDtypeStruct((m, VEXT_WIDTH), BF16),
            jax.ShapeDtypeStruct((m // tm, tm // BLK, GROUP), F32),
        ],
        compiler_params=pltpu.CompilerParams(
            dimension_semantics=("arbitrary",), vmem_limit_bytes=VMEM_LIMIT),
        name="inproj",
    )(x2d, g, w_bf16)


def _bias_kernel(tab_ref, bkt_ref, out_ref):
    g = pl.program_id(0)
    for half in range(2):
        head = jnp.where(g < DIFF_HEADS, g, 2 * g - DIFF_HEADS + half)
        for t in range(2):
            bk = bkt_ref[t]
            vec = jnp.full(bk.shape, -jnp.inf, F32)
            for b in range(N_BUCKETS):
                vec = jnp.where(bk == b, tab_ref[b, head] * LOG2E, vec)
            rows = jnp.broadcast_to(vec, (BLK, 2 * BLK))
            rolled = pltpu.roll(rows, 1, 1, stride=1, stride_axis=0)
            out_ref[0, t, half * BLK:(half + 1) * BLK, :] = rolled[:, BLK:]
        out_ref[0, 2, half * BLK:(half + 1) * BLK, :] = jnp.full((BLK, BLK), tab_ref[N_BUCKETS - 1, head] * LOG2E, F32)


def _bias_tiles(rel_bias):
    bkt = jnp.asarray(_bucket_vectors())
    ngroups = DIFF_HEADS + MOBA_HEADS // 2
    return pl.pallas_call(
        _bias_kernel,
        grid=(ngroups,),
        in_specs=[
            pl.BlockSpec(memory_space=pltpu.SMEM),
            pl.BlockSpec((2, 1, 2 * BLK), lambda g: (0, 0, 0)),
        ],
        out_specs=pl.BlockSpec((1, 3, 2 * BLK, BLK), lambda g: (g, 0, 0, 0)),
        out_shape=jax.ShapeDtypeStruct((ngroups, 3, 2 * BLK, BLK), F32),
        compiler_params=pltpu.CompilerParams(dimension_semantics=("arbitrary",)),
        name="bias_tiles",
    )(rel_bias, bkt)


def _slot_plan(pa, k):
    if k <= pa:
        return 0, k, min(pa - k, 2)
    kb = k - pa - 1
    return 1, kb, min(NBLK - 1 - pa - kb, 2)


def _emit_streams(scr, finish, scores=None, weighted=None):
    qe_scr, s_scr, m_scr = scr
    for j in range(GROUP_PAIRS):
        m_run = [None, None]
        acc_run = [None, None]
        for k in range(NSLOT):
            if scores is not None:
                g, k_at, bias_ref = scores
                pa = PAIR_GROUPS[g][j]
                sel, kb, idx = _slot_plan(pa, k)
                s = lax.dot_general(qe_scr[g, j, sel], k_at(kb), _NT, preferred_element_type=F32)
                s = s + bias_ref[0, idx]
                s_scr[g, j, k] = s
                t = jnp.maximum(s[:, :LANES], s[:, LANES:])
                m_run[sel] = t if m_run[sel] is None else jnp.maximum(m_run[sel], t)
                if kb == (pa, NBLK - 1 - pa)[sel]:
                    m_row = jnp.max(m_run[sel], axis=-1, keepdims=True)
                    m_scr[g, j, sel] = jnp.broadcast_to(m_row, (2 * BLK, LANES))
            if weighted is not None:
                g, v_at = weighted
                pa = PAIR_GROUPS[g][j]
                sel, kb, _ = _slot_plan(pa, k)
                mb = m_scr[g, j, sel]
                s = s_scr[g, j, k]
                p = jnp.concatenate([jnp.exp2(s[:, :LANES] - mb), jnp.exp2(s[:, LANES:] - mb)], axis=1).astype(BF16)
                pv = jnp.dot(p, v_at(kb), preferred_element_type=F32)
                acc_run[sel] = pv if acc_run[sel] is None else acc_run[sel] + pv
                if kb == (pa, NBLK - 1 - pa)[sel]:
                    acc = acc_run[sel]
                    finish((pa, NBLK - 1 - pa)[sel], acc[:, :LANES] / acc[:, LANES:])


def _region(trips, fn):
    def body(i, carry):
        fn()
        return carry
    lax.fori_loop(0, trips, body, 0)


def _pipelined_step(build_cur, build_nxt, k_cur, k_nxt, v_cur, bias_cur, bias_nxt, finish, scr):
    first = (pl.program_id(0) == 0) & (pl.program_id(1) == 0)
    one = jnp.minimum(pl.program_id(0) + 1, 1)

    def prologue():
        build_cur(0)
        build_cur(1)

    def prologue_scores():
        _emit_streams(scr, finish, scores=(0, k_cur, bias_cur))

    def region1():
        build_nxt(0)
        _emit_streams(scr, finish, scores=(1, k_cur, bias_cur), weighted=(0, v_cur))

    def region2():
        build_nxt(1)
        _emit_streams(scr, finish, scores=(0, k_nxt, bias_nxt), weighted=(1, v_cur))

    _region(jnp.where(first, 1, 0), prologue)
    _region(jnp.where(first, 1, 0), prologue_scores)
    _region(one, region1)
    _region(one, region2)


def _silu(g):
    return g / (1.0 + jnp.exp(-g))


def _core_scratch(kc):
    ng = len(PAIR_GROUPS)
    return [
        pltpu.VMEM((ng, GROUP_PAIRS, 2, 2 * BLK, kc), BF16),
        pltpu.VMEM((ng, GROUP_PAIRS, NSLOT, 2 * BLK, BLK), F32),
        pltpu.VMEM((ng, GROUP_PAIRS, 2, 2 * BLK, LANES), F32),
    ]


def _step_maps(nb, ncol):
    def cur(bi, h):
        return bi, h

    def nxt(bi, h):
        flat = jnp.minimum(bi * ncol + h + 1, nb * ncol - 1)
        return flat // ncol, flat % ncol

    return cur, nxt


def _seq_spec(width, which, col0):
    def index_map(bi, h):
        b2, h2 = which(bi, h)
        return b2, 0, 0, col0 + h2
    return pl.BlockSpec((1, NBLK, BLK, width), index_map)


def _bias_spec(which, bias0):
    def index_map(bi, h):
        _, h2 = which(bi, h)
        return bias0 + h2, 0, 0, 0
    return pl.BlockSpec((1, 3, 2 * BLK, BLK), index_map)


def _diff_kernel(q_cur, q_nxt, k_cur, k_nxt, v_cur, g_cur, bias_cur, bias_nxt, lam_ref, sg_ref, o_ref, *scr):
    qe_scr = scr[0]
    lane = lax.broadcasted_iota(jnp.int32, (BLK, LANES), 1)

    def build(q_ref, g):
        for j, pa in enumerate(PAIR_GROUPS[g]):
            for t, qt in enumerate((pa, NBLK - 1 - pa)):
                q = q_ref[0, qt].astype(F32)
                qe_scr[g, j, t, :BLK, :] = jnp.where(lane < DIFF_QK, q, 0.0).astype(BF16)
                qe_scr[g, j, t, BLK:, :] = jnp.where(lane >= DIFF_QK, q, 0.0).astype(BF16)

    def finish(qt, o):
        lp = lam_ref[...]
        lambda_init = 0.8 - 0.6 * math.exp(-0.3 * 0)
        lam = (jnp.exp(jnp.sum(lp[0:1] * lp[1:2], axis=-1, keepdims=True))
               - jnp.exp(jnp.sum(lp[2:3] * lp[3:4], axis=-1, keepdims=True)) + lambda_init)
        out = o[:BLK] - lam * o[BLK:]
        ms = jnp.mean(out * out, axis=-1, keepdims=True)
        out = out * lax.rsqrt(ms + SUBLN_EPS) * sg_ref[...] * (1.0 - lambda_init)
        out = out * _silu(g_cur[0, qt].astype(F32))
        o_ref[qt * BLK:(qt + 1) * BLK, :] = out.astype(BF16)

    _pipelined_step(lambda g: build(q_cur, g), lambda g: build(q_nxt, g),
                    lambda kb: k_cur[0, kb], lambda kb: k_nxt[0, kb], lambda kb: v_cur[0, kb],
                    bias_cur, bias_nxt, finish, scr)


def _diff_attention(qkg4, vext4, bias, diff_lambda, subln_g):
    nb = qkg4.shape[0]
    grp = GROUP // LANES
    cur, nxt = _step_maps(nb, DIFF_HEADS)
    const = lambda bi, h: (0, 0)
    return pl.pallas_call(
        _diff_kernel,
        grid=(nb, DIFF_HEADS),
        in_specs=[_seq_spec(LANES, cur, 0), _seq_spec(LANES, nxt, 0),
                  _seq_spec(LANES, cur, grp), _seq_spec(LANES, nxt, grp),
                  _seq_spec(2 * LANES, cur, 0), _seq_spec(LANES, cur, 2 * grp),
                  _bias_spec(cur, 0), _bias_spec(nxt, 0),
                  pl.BlockSpec((4, DIFF_QK), const), pl.BlockSpec((1, LANES), const)],
        out_specs=pl.BlockSpec((NBLK * BLK, LANES), lambda bi, h: (bi, h)),
        out_shape=jax.ShapeDtypeStruct((nb * NBLK * BLK, DIFF_HEADS * LANES), BF16),
        scratch_shapes=_core_scratch(LANES),
        compiler_params=pltpu.CompilerParams(
            dimension_semantics=("arbitrary", "arbitrary"), vmem_limit_bytes=VMEM_LIMIT),
        name="diff_attention",
    )(qkg4, qkg4, qkg4, qkg4, vext4, qkg4, bias, bias, diff_lambda, subln_g)


def _moba_kernel(q_cur, q_nxt, k_cur, k_nxt, v_cur, g_cur, bias_cur, bias_nxt, kmean_cur, kmean_nxt, o_ref,
                 wg_scr, *scr):
    qe_scr = scr[0]
    lane = lax.broadcasted_iota(jnp.int32, (BLK, LANES), 1)
    n_iota = lax.broadcasted_iota(jnp.int32, (NBLK, BLK), 0)

    def gate_operand(kmean_ref):
        wrow = lax.broadcasted_iota(jnp.int32, (LANES, LANES), 0)
        wlane = lax.broadcasted_iota(jnp.int32, (LANES, LANES), 1)
        wg = jnp.zeros((LANES, LANES), F32)
        per_tile = kmean_ref.shape[1]
        for n in range(NBLK):
            kmean = kmean_ref[n // per_tile, n % per_tile:n % per_tile + 1, :]
            hi = kmean.astype(BF16).astype(F32)
            mid = (kmean - hi).astype(BF16).astype(F32)
            lo = (kmean - hi - mid).astype(BF16).astype(F32)
            for piece, val in enumerate((hi, mid, lo)):
                for hh in range(2):
                    r = piece * 2 * NBLK + hh * NBLK + n
                    head_lanes = (wlane >= hh * MOBA_DIM) & (wlane < (hh + 1) * MOBA_DIM)
                    wg = jnp.where((wrow == r) & head_lanes, val, wg)
        return wg.astype(BF16)

    def build(q_ref, wg, g):
        for j, pa in enumerate(PAIR_GROUPS[g]):
            for t, qt in enumerate((pa, NBLK - 1 - pa)):
                q = q_ref[0, qt]
                qf = q.astype(F32)
                if qt == 0:
                    mask = jnp.zeros((BLK, LANES), F32)
                else:
                    g_t = lax.dot_general(wg, q, _NT, preferred_element_type=F32)
                    g_t = g_t[0:2 * NBLK] + g_t[2 * NBLK:4 * NBLK] + g_t[4 * NBLK:6 * NBLK]
                    mask_rows = []
                    for hh in range(2):
                        g8 = g_t[hh * NBLK:(hh + 1) * NBLK]
                        cnt = jnp.zeros((NBLK, BLK), jnp.int32)
                        for i in range(qt):
                            gi = g8[i:i + 1]
                            beats = (gi > g8) | ((gi == g8) & (i < n_iota))
                            cnt = cnt + jnp.where(beats, 1, 0)
                        keep = ((n_iota < qt) & (cnt < MOBA_TOPK)) | (n_iota == qt)
                        mask_rows.append(jnp.where(keep, 0.0, -MASK_BIG))
                    mask_t = jnp.concatenate(mask_rows + [jnp.zeros((LANES - 2 * NBLK, BLK), F32)], axis=0)
                    mask = mask_t.T
                for hh in range(2):
                    head_lanes = (lane >= hh * MOBA_DIM) & (lane < (hh + 1) * MOBA_DIM)
                    mask_lanes = (lane >= hh * NBLK) & (lane < (hh + 1) * NBLK)
                    qe_scr[g, j, t, hh * BLK:(hh + 1) * BLK, :LANES] = jnp.where(head_lanes, qf, 0.0).astype(BF16)
                    qe_scr[g, j, t, hh * BLK:(hh + 1) * BLK, LANES:] = jnp.where(mask_lanes, mask, 0.0).astype(BF16)

    def build_cur(g):
        build(q_cur, gate_operand(kmean_cur), g)

    def build_nxt(g):
        if g == 0:
            wg = gate_operand(kmean_nxt)
            wg_scr[...] = wg
        else:
            wg = wg_scr[...]
        build(q_nxt, wg, g)

    def finish(qt, o):
        out = jnp.where(lane < MOBA_DIM, o[:BLK], o[BLK:])
        out = out * _silu(g_cur[0, qt].astype(F32))
        o_ref[qt * BLK:(qt + 1) * BLK, :] = out.astype(BF16)

    _pipelined_step(build_cur, build_nxt,
                    lambda kb: k_cur[0, kb], lambda kb: k_nxt[0, kb], lambda kb: v_cur[0, kb],
                    bias_cur, bias_nxt, finish, scr)


def _moba_attention(qkg4, kext4, vext4, kmean, bias):
    nb = qkg4.shape[0]
    grp = GROUP // LANES
    pairs = MOBA_HEADS // 2
    cur, nxt = _step_maps(nb, pairs)
    tiles = kmean.shape[0] // nb

    def kmean_spec(which):
        def index_map(bi, h):
            b2, h2 = which(bi, h)
            return b2, 0, h2
        return pl.BlockSpec((tiles, kmean.shape[1], LANES), index_map)

    return pl.pallas_call(
        _moba_kernel,
        grid=(nb, pairs),
        in_specs=[_seq_spec(LANES, cur, 3 * grp), _seq_spec(LANES, nxt, 3 * grp),
                  _seq_spec(2 * LANES, cur, 0), _seq_spec(2 * LANES, nxt, 0),
                  _seq_spec(2 * LANES, cur, grp), _seq_spec(LANES, cur, 4 * grp),
                  _bias_spec(cur, DIFF_HEADS), _bias_spec(nxt, DIFF_HEADS),
                  kmean_spec(cur), kmean_spec(nxt)],
        out_specs=pl.BlockSpec((NBLK * BLK, LANES), lambda bi, h: (bi, h)),
        out_shape=jax.ShapeDtypeStruct((nb * NBLK * BLK, pairs * LANES), BF16),
        scratch_shapes=[pltpu.VMEM((LANES, LANES), BF16)] + _core_scratch(2 * LANES),
        compiler_params=pltpu.CompilerParams(
            dimension_semantics=("arbitrary", "arbitrary"), vmem_limit_bytes=VMEM_LIMIT),
        name="moba_attention",
    )(qkg4, qkg4, kext4, kext4, vext4, qkg4, bias, bias, kmean, kmean)


def _outproj_kernel(a_ref, b_ref, x_ref, g_ref, w_ref, o_ref):
    half = a_ref.shape[1]
    y = jnp.dot(a_ref[...], w_ref[:half, :], preferred_element_type=F32)
    y = y + jnp.dot(b_ref[...], w_ref[half:, :], preferred_element_type=F32)
    ms = jnp.mean(y * y, axis=-1, keepdims=True)
    o_ref[...] = x_ref[...] + y * lax.rsqrt(ms + NORM_EPS) * g_ref[...]


def _outproj(a, b, x2d, g, w_bf16, tm=1024):
    m = x2d.shape[0]
    half = a.shape[1]
    return pl.pallas_call(
        _outproj_kernel,
        grid=(m // tm,),
        in_specs=[
            pl.BlockSpec((tm, half), lambda i: (i, 0)),
            pl.BlockSpec((tm, half), lambda i: (i, 0)),
            pl.BlockSpec((tm, D_MODEL), lambda i: (i, 0)),
            pl.BlockSpec((1, D_MODEL), lambda i: (0, 0)),
            pl.BlockSpec((2 * half, D_MODEL), lambda i: (0, 0)),
        ],
        out_specs=pl.BlockSpec((tm, D_MODEL), lambda i: (i, 0)),
        out_shape=jax.ShapeDtypeStruct((m, D_MODEL), F32),
        compiler_params=pltpu.CompilerParams(
            dimension_semantics=("arbitrary",), vmem_limit_bytes=VMEM_LIMIT),
        name="outproj",
    )(a, b, x2d, g, w_bf16)


def kernel(x, norm_pre_g, w_in, diff_lambda, diff_subln_g, w_out, norm_post_g, rel_bias):
    b, s, d = x.shape
    assert d == D_MODEL and s == NBLK * BLK
    assert norm_pre_g.shape[0] == 1, "single-layer block"
    x2d = x.reshape(b * s, d)
    qkg, kext, vext, kmean = _inproj(x2d, norm_pre_g[0:1], w_in[0].astype(BF16))
    qkg4 = qkg.reshape(b, NBLK, BLK, QKG_WIDTH)
    kext4 = kext.reshape(b, NBLK, BLK, KEXT_WIDTH)
    vext4 = vext.reshape(b, NBLK, BLK, VEXT_WIDTH)
    bias = _bias_tiles(rel_bias)
    a_out = _diff_attention(qkg4, vext4, bias, diff_lambda[0], diff_subln_g[0:1])
    b_out = _moba_attention(qkg4, kext4, vext4, kmean, bias)
    out = _outproj(a_out, b_out, x2d, norm_post_g[0:1], w_out[0].astype(BF16))
    return out.reshape(b, s, d)
```

```python
import math

import numpy as np
import jax
import jax.numpy as jnp
from jax import lax
from jax.experimental import pallas as pl
from jax.experimental.pallas import tpu as pltpu

F32 = jnp.float32
BF16 = jnp.bfloat16

D_MODEL = 1024
DIFF_HEADS = 4
DIFF_QK = 64
MOBA_HEADS = 8
MOBA_DIM = 64
MOBA_TOPK = 3
N_BUCKETS = 32
MAX_DISTANCE = 128
IN_WIDTH = 4096
GROUP = 512
BLK = 256
NBLK = 8
NSLOT = NBLK + 1
PAIR_GROUPS = ((0, 1), (2, 3))
GROUP_PAIRS = 2
LANES = 128
NORM_EPS = 1e-6
SUBLN_EPS = 1e-5
QK_SCALE = 0.125
LOG2E = math.log2(math.e)
MASK_BIG = 1e30
VMEM_LIMIT = 56 * 1024 * 1024

QKG_WIDTH = 5 * GROUP
KEXT_WIDTH = 2 * GROUP
VEXT_WIDTH = 4 * GROUP

_NT = (((1,), (1,)), ((), ()))


def _rel_bucket(n):
    max_exact = N_BUCKETS // 2
    nf = np.maximum(n, 1).astype(np.float32)
    ratio = np.log(nf / np.float32(max_exact)) / np.float32(math.log(MAX_DISTANCE / max_exact))
    large = max_exact + (ratio * np.float32(N_BUCKETS - max_exact)).astype(np.int32)
    large = np.minimum(large, N_BUCKETS - 1)
    return np.where(n < max_exact, n, large).astype(np.int32)


def _bucket_vectors():
    l = np.arange(2 * BLK)
    out = []
    for t in range(2):
        d = t * BLK + BLK - 1 - l
        out.append(np.where(d >= 0, _rel_bucket(np.maximum(d, 0)), -1).astype(np.int32))
    return np.stack(out)[:, None, :]


def _inproj_kernel(x_ref, g_ref, w_ref, qkg_ref, kext_ref, vext_ref, kmean_ref, wbf_scr):
    tm = x_ref.shape[0]

    @pl.when(pl.program_id(0) == 0)
    def _cast_weights():
        for j in range(IN_WIDTH // GROUP):
            wbf_scr[:, j * GROUP:(j + 1) * GROUP] = w_ref[:, j * GROUP:(j + 1) * GROUP].astype(BF16)

    x = x_ref[...]
    ms = jnp.mean(x * x, axis=-1, keepdims=True)
    h = (x * lax.rsqrt(ms + NORM_EPS) * g_ref[...]).astype(BF16)

    def project(j):
        return jnp.dot(h, wbf_scr[:, j * GROUP:(j + 1) * GROUP], preferred_element_type=F32)

    qkg_ref[:, 0 * GROUP:1 * GROUP] = (project(0) * (QK_SCALE * LOG2E)).astype(BF16)
    qkg_ref[:, 1 * GROUP:2 * GROUP] = project(1).astype(BF16)
    qkg_ref[:, 2 * GROUP:3 * GROUP] = project(3).astype(BF16)
    qkg_ref[:, 3 * GROUP:4 * GROUP] = (project(4) * (QK_SCALE * LOG2E)).astype(BF16)
    qkg_ref[:, 4 * GROUP:5 * GROUP] = project(7).astype(BF16)

    ones = jnp.ones((tm, LANES), BF16)
    for base, j in ((0, 2), (GROUP // LANES, 6)):
        v = project(j).astype(BF16)
        for c in range(GROUP // LANES):
            vext_ref[:, (base + c) * 2 * LANES:(base + c) * 2 * LANES + LANES] = v[:, c * LANES:(c + 1) * LANES]
            vext_ref[:, (base + c) * 2 * LANES + LANES:(base + c + 1) * 2 * LANES] = ones

    mk = project(5)
    lane = lax.broadcasted_iota(jnp.int32, (BLK, LANES), 1)
    blocks_per_tile = tm // BLK
    first_block = (pl.program_id(0) * blocks_per_tile) % NBLK
    for r in range(blocks_per_tile):
        blk = mk[r * BLK:(r + 1) * BLK]
        kmean_ref[0, r:r + 1, :] = jnp.sum(blk, axis=0, keepdims=True) * (1.0 / BLK)
        n = first_block + r
        onehot = jnp.where((lane == n) | (lane == NBLK + n), 1.0, 0.0).astype(BF16)
        for p in range(GROUP // LANES):
            kext_ref[r * BLK:(r + 1) * BLK, p * 2 * LANES:p * 2 * LANES + LANES] = (
                blk[:, p * LANES:(p + 1) * LANES].astype(BF16))
            kext_ref[r * BLK:(r + 1) * BLK, p * 2 * LANES + LANES:(p + 1) * 2 * LANES] = onehot


def _inproj(x2d, g, w_f32, tm=2 * BLK):
    m = x2d.shape[0]
    row = lambda i: (i, 0)
    return pl.pallas_call(
        _inproj_kernel,
        grid=(m // tm,),
        in_specs=[
            pl.BlockSpec((tm, D_MODEL), row),
            pl.BlockSpec((1, D_MODEL), lambda i: (0, 0)),
            pl.BlockSpec((D_MODEL, IN_WIDTH), lambda i: (0, 0), pipeline_mode=pl.Buffered(1)),
        ],
        out_specs=[
            pl.BlockSpec((tm, QKG_WIDTH), row),
            pl.BlockSpec((tm, KEXT_WIDTH), row),
            pl.BlockSpec((tm, VEXT_WIDTH), row),
            pl.BlockSpec((1, tm // BLK, GROUP), lambda i: (i, 0, 0)),
        ],
        out_shape=[
            jax.ShapeDtypeStruct((m, QKG_WIDTH), BF16),
            jax.ShapeDtypeStruct((m, KEXT_WIDTH), BF16),
            jax.ShapeDtypeStruct((m, VEXT_WIDTH), BF16),
            jax.ShapeDtypeStruct((m // tm, tm // BLK, GROUP), F32),
        ],
        scratch_shapes=[pltpu.VMEM((D_MODEL, IN_WIDTH), BF16)],
        compiler_params=pltpu.CompilerParams(
            dimension_semantics=("arbitrary",), vmem_limit_bytes=VMEM_LIMIT),
        name="inproj",
    )(x2d, g, w_f32)


def _bias_kernel(tab_ref, bkt_ref, out_ref):
    g = pl.program_id(0)
    for half in range(2):
        head = jnp.where(g < DIFF_HEADS, g, 2 * g - DIFF_HEADS + half)
        for t in range(2):
            bk = bkt_ref[t]
            vec = jnp.full(bk.shape, -jnp.inf, F32)
            for b in range(N_BUCKETS):
                vec = jnp.where(bk == b, tab_ref[b, head] * LOG2E, vec)
            rows = jnp.broadcast_to(vec, (BLK, 2 * BLK))
            rolled = pltpu.roll(rows, 1, 1, stride=1, stride_axis=0)
            out_ref[0, t, half * BLK:(half + 1) * BLK, :] = rolled[:, BLK:]
        out_ref[0, 2, half * BLK:(half + 1) * BLK, :] = jnp.full((BLK, BLK), tab_ref[N_BUCKETS - 1, head] * LOG2E, F32)


def _bias_tiles(rel_bias):
    bkt = jnp.asarray(_bucket_vectors())
    ngroups = DIFF_HEADS + MOBA_HEADS // 2
    return pl.pallas_call(
        _bias_kernel,
        grid=(ngroups,),
        in_specs=[
            pl.BlockSpec(memory_space=pltpu.SMEM),
            pl.BlockSpec((2, 1, 2 * BLK), lambda g: (0, 0, 0)),
        ],
        out_specs=pl.BlockSpec((1, 3, 2 * BLK, BLK), lambda g: (g, 0, 0, 0)),
        out_shape=jax.ShapeDtypeStruct((ngroups, 3, 2 * BLK, BLK), F32),
        compiler_params=pltpu.CompilerParams(dimension_semantics=("arbitrary",)),
        name="bias_tiles",
    )(rel_bias, bkt)


def _slot_plan(pa, k):
    if k <= pa:
        return 0, k, min(pa - k, 2)
    kb = k - pa - 1
    return 1, kb, min(NBLK - 1 - pa - kb, 2)


def _emit_streams(scr, finish, scores=None, weighted=None):
    qe_scr, s_scr, m_scr = scr
    for j in range(GROUP_PAIRS):
        m_run = [None, None]
        acc_run = [None, None]
        for k in range(NSLOT):
            if scores is not None:
                g, k_at, bias_ref = scores
                pa = PAIR_GROUPS[g][j]
                sel, kb, idx = _slot_plan(pa, k)
                s = lax.dot_general(qe_scr[g, j, sel], k_at(kb), _NT, preferred_element_type=F32)
                s = s + bias_ref[0, idx]
                s_scr[g, j, k] = s
                t = jnp.maximum(s[:, :LANES], s[:, LANES:])
                m_run[sel] = t if m_run[sel] is None else jnp.maximum(m_run[sel], t)
                if kb == (pa, NBLK - 1 - pa)[sel]:
                    m_row = jnp.max(m_run[sel], axis=-1, keepdims=True)
                    m_scr[g, j, sel] = jnp.broadcast_to(m_row, (2 * BLK, LANES))
            if weighted is not None:
                g, v_at = weighted
                pa = PAIR_GROUPS[g][j]
                sel, kb, _ = _slot_plan(pa, k)
                mb = m_scr[g, j, sel]
                s = s_scr[g, j, k]
                p = jnp.concatenate([jnp.exp2(s[:, :LANES] - mb), jnp.exp2(s[:, LANES:] - mb)], axis=1).astype(BF16)
                pv = jnp.dot(p, v_at(kb), preferred_element_type=F32)
                acc_run[sel] = pv if acc_run[sel] is None else acc_run[sel] + pv
                if kb == (pa, NBLK - 1 - pa)[sel]:
                    acc = acc_run[sel]
                    finish((pa, NBLK - 1 - pa)[sel], acc[:, :LANES] / acc[:, LANES:])


def _region(trips, fn):
    def body(i, carry):
        fn()
        return carry
    lax.fori_loop(0, trips, body, 0)


def _pipelined_step(build_cur, build_nxt, k_cur, k_nxt, v_cur, bias_cur, bias_nxt, finish, scr):
    first = (pl.program_id(0) == 0) & (pl.program_id(1) == 0)
    one = jnp.minimum(pl.program_id(0) + 1, 1)

    def prologue():
        build_cur(0)
        build_cur(1)

    def prologue_scores():
        _emit_streams(scr, finish, scores=(0, k_cur, bias_cur))

    def region1():
        build_nxt(0)
        _emit_streams(scr, finish, scores=(1, k_cur, bias_cur), weighted=(0, v_cur))

    def region2():
        build_nxt(1)
        _emit_streams(scr, finish, scores=(0, k_nxt, bias_nxt), weighted=(1, v_cur))

    _region(jnp.where(first, 1, 0), prologue)
    _region(jnp.where(first, 1, 0), prologue_scores)
    _region(one, region1)
    _region(one, region2)


def _silu(g):
    return g / (1.0 + jnp.exp(-g))


def _core_scratch(kc):
    ng = len(PAIR_GROUPS)
    return [
        pltpu.VMEM((ng, GROUP_PAIRS, 2, 2 * BLK, kc), BF16),
        pltpu.VMEM((ng, GROUP_PAIRS, NSLOT, 2 * BLK, BLK), F32),
        pltpu.VMEM((ng, GROUP_PAIRS, 2, 2 * BLK, LANES), F32),
    ]


def _step_maps(nb, ncol):
    def cur(bi, h):
        return bi, h

    def nxt(bi, h):
        flat = jnp.minimum(bi * ncol + h + 1, nb * ncol - 1)
        return flat // ncol, flat % ncol

    return cur, nxt


def _seq_spec(width, which, col0):
    def index_map(bi, h):
        b2, h2 = which(bi, h)
        return b2, 0, 0, col0 + h2
    return pl.BlockSpec((1, NBLK, BLK, width), index_map)


def _bias_spec(which, bias0):
    def index_map(bi, h):
        _, h2 = which(bi, h)
        return bias0 + h2, 0, 0, 0
    return pl.BlockSpec((1, 3, 2 * BLK, BLK), index_map)


def _diff_kernel(q_cur, q_nxt, k_cur, k_nxt, v_cur, g_cur, bias_cur, bias_nxt, lam_ref, sg_ref, o_ref, *scr):
    qe_scr = scr[0]
    lane = lax.broadcasted_iota(jnp.int32, (BLK, LANES), 1)

    def build(q_ref, g):
        for j, pa in enumerate(PAIR_GROUPS[g]):
            for t, qt in enumerate((pa, NBLK - 1 - pa)):
                q = q_ref[0, qt].astype(F32)
                qe_scr[g, j, t, :BLK, :] = jnp.where(lane < DIFF_QK, q, 0.0).astype(BF16)
                qe_scr[g, j, t, BLK:, :] = jnp.where(lane >= DIFF_QK, q, 0.0).astype(BF16)

    def finish(qt, o):
        lp = lam_ref[...]
        lambda_init = 0.8 - 0.6 * math.exp(-0.3 * 0)
        lam = (jnp.exp(jnp.sum(lp[0:1] * lp[1:2], axis=-1, keepdims=True))
               - jnp.exp(jnp.sum(lp[2:3] * lp[3:4], axis=-1, keepdims=True)) + lambda_init)
        out = o[:BLK] - lam * o[BLK:]
        ms = jnp.mean(out * out, axis=-1, keepdims=True)
        out = out * lax.rsqrt(ms + SUBLN_EPS) * sg_ref[...] * (1.0 - lambda_init)
        out = out * _silu(g_cur[0, qt].astype(F32))
        o_ref[qt * BLK:(qt + 1) * BLK, :] = out.astype(BF16)

    _pipelined_step(lambda g: build(q_cur, g), lambda g: build(q_nxt, g),
                    lambda kb: k_cur[0, kb], lambda kb: k_nxt[0, kb], lambda kb: v_cur[0, kb],
                    bias_cur, bias_nxt, finish, scr)


def _diff_attention(qkg4, vext4, bias, diff_lambda, subln_g):
    nb = qkg4.shape[0]
    grp = GROUP // LANES
    cur, nxt = _step_maps(nb, DIFF_HEADS)
    const = lambda bi, h: (0, 0)
    return pl.pallas_call(
        _diff_kernel,
        grid=(nb, DIFF_HEADS),
        in_specs=[_seq_spec(LANES, cur, 0), _seq_spec(LANES, nxt, 0),
                  _seq_spec(LANES, cur, grp), _seq_spec(LANES, nxt, grp),
                  _seq_spec(2 * LANES, cur, 0), _seq_spec(LANES, cur, 2 * grp),
                  _bias_spec(cur, 0), _bias_spec(nxt, 0),
                  pl.BlockSpec((4, DIFF_QK), const), pl.BlockSpec((1, LANES), const)],
        out_specs=pl.BlockSpec((NBLK * BLK, LANES), lambda bi, h: (bi, h)),
        out_shape=jax.ShapeDtypeStruct((nb * NBLK * BLK, DIFF_HEADS * LANES), BF16),
        scratch_shapes=_core_scratch(LANES),
        compiler_params=pltpu.CompilerParams(
            dimension_semantics=("arbitrary", "arbitrary"), vmem_limit_bytes=VMEM_LIMIT),
        name="diff_attention",
    )(qkg4, qkg4, qkg4, qkg4, vext4, qkg4, bias, bias, diff_lambda, subln_g)


def _moba_kernel(q_cur, q_nxt, k_cur, k_nxt, v_cur, g_cur, bias_cur, bias_nxt, kmean_cur, kmean_nxt, o_ref,
                 wg_scr, *scr):
    qe_scr = scr[0]
    lane = lax.broadcasted_iota(jnp.int32, (BLK, LANES), 1)
    n_iota = lax.broadcasted_iota(jnp.int32, (NBLK, BLK), 0)

    def gate_operand(kmean_ref):
        wrow = lax.broadcasted_iota(jnp.int32, (LANES, LANES), 0)
        wlane = lax.broadcasted_iota(jnp.int32, (LANES, LANES), 1)
        wg = jnp.zeros((LANES, LANES), F32)
        per_tile = kmean_ref.shape[1]
        for n in range(NBLK):
            kmean = kmean_ref[n // per_tile, n % per_tile:n % per_tile + 1, :]
            hi = kmean.astype(BF16).astype(F32)
            mid = (kmean - hi).astype(BF16).astype(F32)
            lo = (kmean - hi - mid).astype(BF16).astype(F32)
            for piece, val in enumerate((hi, mid, lo)):
                for hh in range(2):
                    r = piece * 2 * NBLK + hh * NBLK + n
                    head_lanes = (wlane >= hh * MOBA_DIM) & (wlane < (hh + 1) * MOBA_DIM)
                    wg = jnp.where((wrow == r) & head_lanes, val, wg)
        return wg.astype(BF16)

    def build(q_ref, wg, g):
        for j, pa in enumerate(PAIR_GROUPS[g]):
            for t, qt in enumerate((pa, NBLK - 1 - pa)):
                q = q_ref[0, qt]
                qf = q.astype(F32)
                if qt == 0:
                    mask = jnp.zeros((BLK, LANES), F32)
                else:
                    g_t = lax.dot_general(wg, q, _NT, preferred_element_type=F32)
                    g_t = g_t[0:2 * NBLK] + g_t[2 * NBLK:4 * NBLK] + g_t[4 * NBLK:6 * NBLK]
                    mask_rows = []
                    for hh in range(2):
                        g8 = g_t[hh * NBLK:(hh + 1) * NBLK]
                        cnt = jnp.zeros((NBLK, BLK), jnp.int32)
                        for i in range(qt):
                            gi = g8[i:i + 1]
                            beats = (gi > g8) | ((gi == g8) & (i < n_iota))
                            cnt = cnt + jnp.where(beats, 1, 0)
                        keep = ((n_iota < qt) & (cnt < MOBA_TOPK)) | (n_iota == qt)
                        mask_rows.append(jnp.where(keep, 0.0, -MASK_BIG))
                    mask_t = jnp.concatenate(mask_rows + [jnp.zeros((LANES - 2 * NBLK, BLK), F32)], axis=0)
                    mask = mask_t.T
                for hh in range(2):
                    head_lanes = (lane >= hh * MOBA_DIM) & (lane < (hh + 1) * MOBA_DIM)
                    mask_lanes = (lane >= hh * NBLK) & (lane < (hh + 1) * NBLK)
                    qe_scr[g, j, t, hh * BLK:(hh + 1) * BLK, :LANES] = jnp.where(head_lanes, qf, 0.0).astype(BF16)
                    qe_scr[g, j, t, hh * BLK:(hh + 1) * BLK, LANES:] = jnp.where(mask_lanes, mask, 0.0).astype(BF16)

    def build_cur(g):
        build(q_cur, gate_operand(kmean_cur), g)

    def build_nxt(g):
        if g == 0:
            wg = gate_operand(kmean_nxt)
            wg_scr[...] = wg
        else:
            wg = wg_scr[...]
        build(q_nxt, wg, g)

    def finish(qt, o):
        out = jnp.where(lane < MOBA_DIM, o[:BLK], o[BLK:])
        out = out * _silu(g_cur[0, qt].astype(F32))
        o_ref[qt * BLK:(qt + 1) * BLK, :] = out.astype(BF16)

    _pipelined_step(build_cur, build_nxt,
                    lambda kb: k_cur[0, kb], lambda kb: k_nxt[0, kb], lambda kb: v_cur[0, kb],
                    bias_cur, bias_nxt, finish, scr)


def _moba_attention(qkg4, kext4, vext4, kmean, bias):
    nb = qkg4.shape[0]
    grp = GROUP // LANES
    pairs = MOBA_HEADS // 2
    cur, nxt = _step_maps(nb, pairs)
    tiles = kmean.shape[0] // nb

    def kmean_spec(which):
        def index_map(bi, h):
            b2, h2 = which(bi, h)
            return b2, 0, h2
        return pl.BlockSpec((tiles, kmean.shape[1], LANES), index_map)

    return pl.pallas_call(
        _moba_kernel,
        grid=(nb, pairs),
        in_specs=[_seq_spec(LANES, cur, 3 * grp), _seq_spec(LANES, nxt, 3 * grp),
                  _seq_spec(2 * LANES, cur, 0), _seq_spec(2 * LANES, nxt, 0),
                  _seq_spec(2 * LANES, cur, grp), _seq_spec(LANES, cur, 4 * grp),
                  _bias_spec(cur, DIFF_HEADS), _bias_spec(nxt, DIFF_HEADS),
                  kmean_spec(cur), kmean_spec(nxt)],
        out_specs=pl.BlockSpec((NBLK * BLK, LANES), lambda bi, h: (bi, h)),
        out_shape=jax.ShapeDtypeStruct((nb * NBLK * BLK, pairs * LANES), BF16),
        scratch_shapes=[pltpu.VMEM((LANES, LANES), BF16)] + _core_scratch(2 * LANES),
        compiler_params=pltpu.CompilerParams(
            dimension_semantics=("arbitrary", "arbitrary"), vmem_limit_bytes=VMEM_LIMIT),
        name="moba_attention",
    )(qkg4, qkg4, kext4, kext4, vext4, qkg4, bias, bias, kmean, kmean)


def _outproj_kernel(a_ref, b_ref, x_ref, g_ref, w_ref, o_ref):
    half = a_ref.shape[1]
    y = jnp.dot(a_ref[...], w_ref[:half, :], preferred_element_type=F32)
    y = y + jnp.dot(b_ref[...], w_ref[half:, :], preferred_element_type=F32)
    ms = jnp.mean(y * y, axis=-1, keepdims=True)
    o_ref[...] = x_ref[...] + y * lax.rsqrt(ms + NORM_EPS) * g_ref[...]


def _outproj(a, b, x2d, g, w_bf16, tm=2048):
    m = x2d.shape[0]
    half = a.shape[1]
    return pl.pallas_call(
        _outproj_kernel,
        grid=(m // tm,),
        in_specs=[
            pl.BlockSpec((tm, half), lambda i: (i, 0)),
            pl.BlockSpec((tm, half), lambda i: (i, 0)),
            pl.BlockSpec((tm, D_MODEL), lambda i: (i, 0)),
            pl.BlockSpec((1, D_MODEL), lambda i: (0, 0)),
            pl.BlockSpec((2 * half, D_MODEL), lambda i: (0, 0)),
        ],
        out_specs=pl.BlockSpec((tm, D_MODEL), lambda i: (i, 0)),
        out_shape=jax.ShapeDtypeStruct((m, D_MODEL), F32),
        compiler_params=pltpu.CompilerParams(
            dimension_semantics=("arbitrary",), vmem_limit_bytes=VMEM_LIMIT),
        name="outproj",
    )(a, b, x2d, g, w_bf16)


def kernel(x, norm_pre_g, w_in, diff_lambda, diff_subln_g, w_out, norm_post_g, rel_bias):
    b, s, d = x.shape
    assert d == D_MODEL and s == NBLK * BLK
    assert norm_pre_g.shape[0] == 1, "single-layer block"
    x2d = x.reshape(b * s, d)
    qkg, kext, vext, kmean = _inproj(x2d, norm_pre_g[0:1], w_in[0])
    qkg4 = qkg.reshape(b, NBLK, BLK, QKG_WIDTH)
    kext4 = kext.reshape(b, NBLK, BLK, KEXT_WIDTH)
    vext4 = vext.reshape(b, NBLK, BLK, VEXT_WIDTH)
    bias = _bias_tiles(rel_bias)
    a_out = _diff_attention(qkg4, vext4, bias, diff_lambda[0], diff_subln_g[0:1])
    b_out = _moba_attention(qkg4, kext4, vext4, kmean, bias)
    out = _outproj(a_out, b_out, x2d, norm_post_g[0:1], w_out[0].astype(BF16))
    return out.reshape(b, s, d)
```

```python
import math

import numpy as np
import jax
import jax.numpy as jnp
from jax import lax
from jax.experimental import pallas as pl
from jax.experimental.pallas import tpu as pltpu

F32 = jnp.float32
BF16 = jnp.bfloat16

D_MODEL = 1024
DIFF_HEADS = 4
DIFF_QK = 64
MOBA_HEADS = 8
MOBA_DIM = 64
MOBA_TOPK = 3
N_BUCKETS = 32
MAX_DISTANCE = 128
IN_WIDTH = 4096
GROUP = 512
BLK = 256
NBLK = 8
NSLOT = NBLK + 1
PAIR_GROUPS = ((0, 1), (2, 3))
GROUP_PAIRS = 2
LANES = 128
NORM_EPS = 1e-6
SUBLN_EPS = 1e-5
QK_SCALE = 0.125
LOG2E = math.log2(math.e)
MASK_BIG = 1e30
VMEM_LIMIT = 56 * 1024 * 1024

QKG_WIDTH = 5 * GROUP
KEXT_WIDTH = 2 * GROUP
VEXT_WIDTH = 4 * GROUP

_NT = (((1,), (1,)), ((), ()))


def _rel_bucket(n):
    max_exact = N_BUCKETS // 2
    nf = np.maximum(n, 1).astype(np.float32)
    ratio = np.log(nf / np.float32(max_exact)) / np.float32(math.log(MAX_DISTANCE / max_exact))
    large = max_exact + (ratio * np.float32(N_BUCKETS - max_exact)).astype(np.int32)
    large = np.minimum(large, N_BUCKETS - 1)
    return np.where(n < max_exact, n, large).astype(np.int32)


def _bucket_vectors():
    l = np.arange(2 * BLK)
    out = []
    for t in range(2):
        d = t * BLK + BLK - 1 - l
        out.append(np.where(d >= 0, _rel_bucket(np.maximum(d, 0)), -1).astype(np.int32))
    return np.stack(out)[:, None, :]


def _inproj_kernel(x_ref, g_ref, w_ref, qkg_ref, kext_ref, vext_ref, kmean_ref, wbf_scr):
    tm = x_ref.shape[0]

    @pl.when(pl.program_id(0) == 0)
    def _cast_weights():
        for j in range(IN_WIDTH // GROUP):
            wbf_scr[:, j * GROUP:(j + 1) * GROUP] = w_ref[:, j * GROUP:(j + 1) * GROUP].astype(BF16)

    x = x_ref[...]
    ms = jnp.mean(x * x, axis=-1, keepdims=True)
    h = (x * lax.rsqrt(ms + NORM_EPS) * g_ref[...]).astype(BF16)

    def project(j):
        return jnp.dot(h, wbf_scr[:, j * GROUP:(j + 1) * GROUP], preferred_element_type=F32)

    qkg_ref[:, 0 * GROUP:1 * GROUP] = (project(0) * (QK_SCALE * LOG2E)).astype(BF16)
    qkg_ref[:, 1 * GROUP:2 * GROUP] = project(1).astype(BF16)
    qkg_ref[:, 2 * GROUP:3 * GROUP] = project(3).astype(BF16)
    qkg_ref[:, 3 * GROUP:4 * GROUP] = (project(4) * (QK_SCALE * LOG2E)).astype(BF16)
    qkg_ref[:, 4 * GROUP:5 * GROUP] = project(7).astype(BF16)

    ones = jnp.ones((tm, LANES), BF16)
    for base, j in ((0, 2), (GROUP // LANES, 6)):
        v = project(j).astype(BF16)
        for c in range(GROUP // LANES):
            vext_ref[:, (base + c) * 2 * LANES:(base + c) * 2 * LANES + LANES] = v[:, c * LANES:(c + 1) * LANES]
            vext_ref[:, (base + c) * 2 * LANES + LANES:(base + c + 1) * 2 * LANES] = ones

    mk = project(5)
    lane = lax.broadcasted_iota(jnp.int32, (BLK, LANES), 1)
    blocks_per_tile = tm // BLK
    first_block = (pl.program_id(0) * blocks_per_tile) % NBLK
    for r in range(blocks_per_tile):
        blk = mk[r * BLK:(r + 1) * BLK]
        kmean_ref[0, r:r + 1, :] = jnp.sum(blk, axis=0, keepdims=True) * (1.0 / BLK)
        n = first_block + r
        onehot = jnp.where((lane == n) | (lane == NBLK + n), 1.0, 0.0).astype(BF16)
        for p in range(GROUP // LANES):
            kext_ref[r * BLK:(r + 1) * BLK, p * 2 * LANES:p * 2 * LANES + LANES] = (
                blk[:, p * LANES:(p + 1) * LANES].astype(BF16))
            kext_ref[r * BLK:(r + 1) * BLK, p * 2 * LANES + LANES:(p + 1) * 2 * LANES] = onehot


def _inproj(x2d, g, w_f32, tm=2 * BLK):
    m = x2d.shape[0]
    row = lambda i: (i, 0)
    return pl.pallas_call(
        _inproj_kernel,
        grid=(m // tm,),
        in_specs=[
            pl.BlockSpec((tm, D_MODEL), row),
            pl.BlockSpec((1, D_MODEL), lambda i: (0, 0)),
            pl.BlockSpec((D_MODEL, IN_WIDTH), lambda i: (0, 0), pipeline_mode=pl.Buffered(1)),
        ],
        out_specs=[
            pl.BlockSpec((tm, QKG_WIDTH), row),
            pl.BlockSpec((tm, KEXT_WIDTH), row),
            pl.BlockSpec((tm, VEXT_WIDTH), row),
            pl.BlockSpec((1, tm // BLK, GROUP), lambda i: (i, 0, 0)),
        ],
        out_shape=[
            jax.ShapeDtypeStruct((m, QKG_WIDTH), BF16),
            jax.ShapeDtypeStruct((m, KEXT_WIDTH), BF16),
            jax.ShapeDtypeStruct((m, VEXT_WIDTH), BF16),
            jax.ShapeDtypeStruct((m // tm, tm // BLK, GROUP), F32),
        ],
        scratch_shapes=[pltpu.VMEM((D_MODEL, IN_WIDTH), BF16)],
        compiler_params=pltpu.CompilerParams(
            dimension_semantics=("arbitrary",), vmem_limit_bytes=VMEM_LIMIT),
        name="inproj",
    )(x2d, g, w_f32)


def _build_bias_tiles(tab_ref, bkt_ref, bias_scr, heads_of_group):
    for g, heads in enumerate(heads_of_group):
        for half, head in enumerate(heads):
            for t in range(2):
                bk = bkt_ref[t]
                vec = jnp.full(bk.shape, -jnp.inf, F32)
                for b in range(N_BUCKETS):
                    vec = jnp.where(bk == b, tab_ref[b, head] * LOG2E, vec)
                rows = jnp.broadcast_to(vec, (BLK, 2 * BLK))
                rolled = pltpu.roll(rows, 1, 1, stride=1, stride_axis=0)
                bias_scr[g, t, half * BLK:(half + 1) * BLK, :] = rolled[:, BLK:]
            bias_scr[g, 2, half * BLK:(half + 1) * BLK, :] = jnp.full(
                (BLK, BLK), tab_ref[N_BUCKETS - 1, head] * LOG2E, F32)


def _slot_plan(pa, k):
    if k <= pa:
        return 0, k, min(pa - k, 2)
    kb = k - pa - 1
    return 1, kb, min(NBLK - 1 - pa - kb, 2)


def _emit_streams(scr, finish, scores=None, weighted=None):
    qe_scr, s_scr, m_scr = scr
    for j in range(GROUP_PAIRS):
        m_run = [None, None]
        acc_run = [None, None]
        for k in range(NSLOT):
            if scores is not None:
                g, k_at, bias_at = scores
                pa = PAIR_GROUPS[g][j]
                sel, kb, idx = _slot_plan(pa, k)
                s = lax.dot_general(qe_scr[g, j, sel], k_at(kb), _NT, preferred_element_type=F32)
                s = s + bias_at(idx)
                s_scr[g, j, k] = s
                t = jnp.maximum(s[:, :LANES], s[:, LANES:])
                m_run[sel] = t if m_run[sel] is None else jnp.maximum(m_run[sel], t)
                if kb == (pa, NBLK - 1 - pa)[sel]:
                    m_row = jnp.max(m_run[sel], axis=-1, keepdims=True)
                    m_scr[g, j, sel] = jnp.broadcast_to(m_row, (2 * BLK, LANES))
            if weighted is not None:
                g, v_at = weighted
                pa = PAIR_GROUPS[g][j]
                sel, kb, _ = _slot_plan(pa, k)
                mb = m_scr[g, j, sel]
                s = s_scr[g, j, k]
                p = jnp.concatenate([jnp.exp2(s[:, :LANES] - mb), jnp.exp2(s[:, LANES:] - mb)], axis=1).astype(BF16)
                pv = jnp.dot(p, v_at(kb), preferred_element_type=F32)
                acc_run[sel] = pv if acc_run[sel] is None else acc_run[sel] + pv
                if kb == (pa, NBLK - 1 - pa)[sel]:
                    acc = acc_run[sel]
                    finish((pa, NBLK - 1 - pa)[sel], acc[:, :LANES] / acc[:, LANES:])


def _region(trips, fn):
    def body(i, carry):
        fn()
        return carry
    lax.fori_loop(0, trips, body, 0)


def _pipelined_step(build_bias, build_cur, build_nxt, k_cur, k_nxt, v_cur, bias_cur, bias_nxt, finish, scr):
    first = (pl.program_id(0) == 0) & (pl.program_id(1) == 0)
    one = jnp.minimum(pl.program_id(0) + 1, 1)

    def prologue():
        build_bias()
        build_cur(0)
        build_cur(1)

    def prologue_scores():
        _emit_streams(scr, finish, scores=(0, k_cur, bias_cur))

    def region1():
        build_nxt(0)
        _emit_streams(scr, finish, scores=(1, k_cur, bias_cur), weighted=(0, v_cur))

    def region2():
        build_nxt(1)
        _emit_streams(scr, finish, scores=(0, k_nxt, bias_nxt), weighted=(1, v_cur))

    _region(jnp.where(first, 1, 0), prologue)
    _region(jnp.where(first, 1, 0), prologue_scores)
    _region(one, region1)
    _region(one, region2)


def _silu(g):
    return g / (1.0 + jnp.exp(-g))


def _core_scratch(kc):
    ng = len(PAIR_GROUPS)
    return [
        pltpu.VMEM((ng, GROUP_PAIRS, 2, 2 * BLK, kc), BF16),
        pltpu.VMEM((ng, GROUP_PAIRS, NSLOT, 2 * BLK, BLK), F32),
        pltpu.VMEM((ng, GROUP_PAIRS, 2, 2 * BLK, LANES), F32),
    ]


def _step_maps(nb, ncol):
    def cur(bi, h):
        return bi, h

    def nxt(bi, h):
        flat = jnp.minimum(bi * ncol + h + 1, nb * ncol - 1)
        return flat // ncol, flat % ncol

    return cur, nxt


def _seq_spec(width, which, col0):
    def index_map(bi, h):
        b2, h2 = which(bi, h)
        return b2, 0, 0, col0 + h2
    return pl.BlockSpec((1, NBLK, BLK, width), index_map)


def _bias_accessors(bias_scr, nb, ncol):
    h = pl.program_id(1)
    _, h_nxt = _step_maps(nb, ncol)[1](pl.program_id(0), h)
    return (lambda idx: bias_scr[h, idx]), (lambda idx: bias_scr[h_nxt, idx])


def _bias_inputs():
    bkt = jnp.asarray(_bucket_vectors())
    specs = [pl.BlockSpec(memory_space=pltpu.SMEM), pl.BlockSpec(bkt.shape, lambda bi, h: (0, 0, 0))]
    return bkt, specs


def _bias_scratch(ncol):
    return pltpu.VMEM((ncol, 3, 2 * BLK, BLK), F32)


def _diff_kernel(tab_ref, bkt_ref, q_cur, q_nxt, k_cur, k_nxt, v_cur, g_cur, lam_ref, sg_ref, o_ref, bias_scr, *scr):
    qe_scr = scr[0]
    bias_cur, bias_nxt = _bias_accessors(bias_scr, pl.num_programs(0), DIFF_HEADS)
    lane = lax.broadcasted_iota(jnp.int32, (BLK, LANES), 1)

    def build(q_ref, g):
        for j, pa in enumerate(PAIR_GROUPS[g]):
            for t, qt in enumerate((pa, NBLK - 1 - pa)):
                q = q_ref[0, qt].astype(F32)
                qe_scr[g, j, t, :BLK, :] = jnp.where(lane < DIFF_QK, q, 0.0).astype(BF16)
                qe_scr[g, j, t, BLK:, :] = jnp.where(lane >= DIFF_QK, q, 0.0).astype(BF16)

    def finish(qt, o):
        lp = lam_ref[...]
        lambda_init = 0.8 - 0.6 * math.exp(-0.3 * 0)
        lam = (jnp.exp(jnp.sum(lp[0:1] * lp[1:2], axis=-1, keepdims=True))
               - jnp.exp(jnp.sum(lp[2:3] * lp[3:4], axis=-1, keepdims=True)) + lambda_init)
        out = o[:BLK] - lam * o[BLK:]
        ms = jnp.mean(out * out, axis=-1, keepdims=True)
        out = out * lax.rsqrt(ms + SUBLN_EPS) * sg_ref[...] * (1.0 - lambda_init)
        out = out * _silu(g_cur[0, qt].astype(F32))
        o_ref[qt * BLK:(qt + 1) * BLK, :] = out.astype(BF16)

    _pipelined_step(lambda: _build_bias_tiles(tab_ref, bkt_ref, bias_scr, [(h, h) for h in range(DIFF_HEADS)]),
                    lambda g: build(q_cur, g), lambda g: build(q_nxt, g),
                    lambda kb: k_cur[0, kb], lambda kb: k_nxt[0, kb], lambda kb: v_cur[0, kb],
                    bias_cur, bias_nxt, finish, scr)


def _diff_attention(qkg4, vext4, rel_bias, diff_lambda, subln_g):
    nb = qkg4.shape[0]
    grp = GROUP // LANES
    cur, nxt = _step_maps(nb, DIFF_HEADS)
    const = lambda bi, h: (0, 0)
    bkt, bias_specs = _bias_inputs()
    return pl.pallas_call(
        _diff_kernel,
        grid=(nb, DIFF_HEADS),
        in_specs=bias_specs + [
            _seq_spec(LANES, cur, 0), _seq_spec(LANES, nxt, 0),
            _seq_spec(LANES, cur, grp), _seq_spec(LANES, nxt, grp),
            _seq_spec(2 * LANES, cur, 0), _seq_spec(LANES, cur, 2 * grp),
            pl.BlockSpec((4, DIFF_QK), const), pl.BlockSpec((1, LANES), const)],
        out_specs=pl.BlockSpec((NBLK * BLK, LANES), lambda bi, h: (bi, h)),
        out_shape=jax.ShapeDtypeStruct((nb * NBLK * BLK, DIFF_HEADS * LANES), BF16),
        scratch_shapes=[_bias_scratch(DIFF_HEADS)] + _core_scratch(LANES),
        compiler_params=pltpu.CompilerParams(
            dimension_semantics=("arbitrary", "arbitrary"), vmem_limit_bytes=VMEM_LIMIT),
        name="diff_attention",
    )(rel_bias, bkt, qkg4, qkg4, qkg4, qkg4, vext4, qkg4, diff_lambda, subln_g)


def _moba_kernel(tab_ref, bkt_ref, q_cur, q_nxt, k_cur, k_nxt, v_cur, g_cur, kmean_cur, kmean_nxt, o_ref,
                 bias_scr, wg_scr, *scr):
    qe_scr = scr[0]
    pairs = MOBA_HEADS // 2
    bias_cur, bias_nxt = _bias_accessors(bias_scr, pl.num_programs(0), pairs)
    lane = lax.broadcasted_iota(jnp.int32, (BLK, LANES), 1)
    n_iota = lax.broadcasted_iota(jnp.int32, (NBLK, BLK), 0)

    def gate_operand(kmean_ref):
        wrow = lax.broadcasted_iota(jnp.int32, (LANES, LANES), 0)
        wlane = lax.broadcasted_iota(jnp.int32, (LANES, LANES), 1)
        wg = jnp.zeros((LANES, LANES), F32)
        per_tile = kmean_ref.shape[1]
        for n in range(NBLK):
            kmean = kmean_ref[n // per_tile, n % per_tile:n % per_tile + 1, :]
            hi = kmean.astype(BF16).astype(F32)
            mid = (kmean - hi).astype(BF16).astype(F32)
            lo = (kmean - hi - mid).astype(BF16).astype(F32)
            for piece, val in enumerate((hi, mid, lo)):
                for hh in range(2):
                    r = piece * 2 * NBLK + hh * NBLK + n
                    head_lanes = (wlane >= hh * MOBA_DIM) & (wlane < (hh + 1) * MOBA_DIM)
                    wg = jnp.where((wrow == r) & head_lanes, val, wg)
        return wg.astype(BF16)

    def build(q_ref, wg, g):
        for j, pa in enumerate(PAIR_GROUPS[g]):
            for t, qt in enumerate((pa, NBLK - 1 - pa)):
                q = q_ref[0, qt]
                qf = q.astype(F32)
                if qt == 0:
                    mask = jnp.zeros((BLK, LANES), F32)
                else:
                    g_t = lax.dot_general(wg, q, _NT, preferred_element_type=F32)
                    g_t = g_t[0:2 * NBLK] + g_t[2 * NBLK:4 * NBLK] + g_t[4 * NBLK:6 * NBLK]
                    mask_rows = []
                    for hh in range(2):
                        g8 = g_t[hh * NBLK:(hh + 1) * NBLK]
                        cnt = jnp.zeros((NBLK, BLK), jnp.int32)
                        for i in range(qt):
                            gi = g8[i:i + 1]
                            beats = (gi > g8) | ((gi == g8) & (i < n_iota))
                            cnt = cnt + jnp.where(beats, 1, 0)
                        keep = ((n_iota < qt) & (cnt < MOBA_TOPK)) | (n_iota == qt)
                        mask_rows.append(jnp.where(keep, 0.0, -MASK_BIG))
                    mask_t = jnp.concatenate(mask_rows + [jnp.zeros((LANES - 2 * NBLK, BLK), F32)], axis=0)
                    mask = mask_t.T
                for hh in range(2):
                    head_lanes = (lane >= hh * MOBA_DIM) & (lane < (hh + 1) * MOBA_DIM)
                    mask_lanes = (lane >= hh * NBLK) & (lane < (hh + 1) * NBLK)
                    qe_scr[g, j, t, hh * BLK:(hh + 1) * BLK, :LANES] = jnp.where(head_lanes, qf, 0.0).astype(BF16)
                    qe_scr[g, j, t, hh * BLK:(hh + 1) * BLK, LANES:] = jnp.where(mask_lanes, mask, 0.0).astype(BF16)

    def build_cur(g):
        build(q_cur, gate_operand(kmean_cur), g)

    def build_nxt(g):
        if g == 0:
            wg = gate_operand(kmean_nxt)
            wg_scr[...] = wg
        else:
            wg = wg_scr[...]
        build(q_nxt, wg, g)

    def finish(qt, o):
        out = jnp.where(lane < MOBA_DIM, o[:BLK], o[BLK:])
        out = out * _silu(g_cur[0, qt].astype(F32))
        o_ref[qt * BLK:(qt + 1) * BLK, :] = out.astype(BF16)

    moba_heads = [(DIFF_HEADS + 2 * p, DIFF_HEADS + 2 * p + 1) for p in range(pairs)]
    _pipelined_step(lambda: _build_bias_tiles(tab_ref, bkt_ref, bias_scr, moba_heads), build_cur, build_nxt,
                    lambda kb: k_cur[0, kb], lambda kb: k_nxt[0, kb], lambda kb: v_cur[0, kb],
                    bias_cur, bias_nxt, finish, scr)


def _moba_attention(qkg4, kext4, vext4, kmean, rel_bias):
    nb = qkg4.shape[0]
    grp = GROUP // LANES
    pairs = MOBA_HEADS // 2
    cur, nxt = _step_maps(nb, pairs)
    tiles = kmean.shape[0] // nb

    def kmean_spec(which):
        def index_map(bi, h):
            b2, h2 = which(bi, h)
            return b2, 0, h2
        return pl.BlockSpec((tiles, kmean.shape[1], LANES), index_map)

    bkt, bias_specs = _bias_inputs()
    return pl.pallas_call(
        _moba_kernel,
        grid=(nb, pairs),
        in_specs=bias_specs + [
            _seq_spec(LANES, cur, 3 * grp), _seq_spec(LANES, nxt, 3 * grp),
            _seq_spec(2 * LANES, cur, 0), _seq_spec(2 * LANES, nxt, 0),
            _seq_spec(2 * LANES, cur, grp), _seq_spec(LANES, cur, 4 * grp),
            kmean_spec(cur), kmean_spec(nxt)],
        out_specs=pl.BlockSpec((NBLK * BLK, LANES), lambda bi, h: (bi, h)),
        out_shape=jax.ShapeDtypeStruct((nb * NBLK * BLK, pairs * LANES), BF16),
        scratch_shapes=[_bias_scratch(pairs), pltpu.VMEM((LANES, LANES), BF16)] + _core_scratch(2 * LANES),
        compiler_params=pltpu.CompilerParams(
            dimension_semantics=("arbitrary", "arbitrary"), vmem_limit_bytes=VMEM_LIMIT),
        name="moba_attention",
    )(rel_bias, bkt, qkg4, qkg4, kext4, kext4, vext4, qkg4, kmean, kmean)


X_SLOTS = 3


def _outproj_kernel(a_ref, b_ref, x_hbm, g_ref, w_ref, o_ref, x_buf, x_sem):
    i = pl.program_id(0)
    nsteps = pl.num_programs(0)
    tm, half = a_ref.shape

    def x_copy(step):
        slot = step % X_SLOTS
        return pltpu.make_async_copy(x_hbm.at[pl.ds(step * tm, tm), :], x_buf.at[slot], x_sem.at[slot])

    @pl.when(i == 0)
    def _start_first():
        x_copy(0).start()

        @pl.when(nsteps > 1)
        def _():
            x_copy(1).start()

    @pl.when(i + 2 < nsteps)
    def _start_ahead():
        x_copy(i + 2).start()

    y = jnp.dot(a_ref[...], w_ref[:half, :], preferred_element_type=F32)
    y = y + jnp.dot(b_ref[...], w_ref[half:, :], preferred_element_type=F32)
    ms = jnp.mean(y * y, axis=-1, keepdims=True)
    y = y * lax.rsqrt(ms + NORM_EPS) * g_ref[...]
    x_copy(i).wait()
    o_ref[...] = x_buf[i % X_SLOTS] + y


def _outproj(a, b, x2d, g, w_bf16, tm=1024):
    m = x2d.shape[0]
    half = a.shape[1]
    return pl.pallas_call(
        _outproj_kernel,
        grid=(m // tm,),
        in_specs=[
            pl.BlockSpec((tm, half), lambda i: (i, 0)),
            pl.BlockSpec((tm, half), lambda i: (i, 0)),
            pl.BlockSpec(memory_space=pl.ANY),
            pl.BlockSpec((1, D_MODEL), lambda i: (0, 0)),
            pl.BlockSpec((2 * half, D_MODEL), lambda i: (0, 0)),
        ],
        out_specs=pl.BlockSpec((tm, D_MODEL), lambda i: (i, 0)),
        out_shape=jax.ShapeDtypeStruct((m, D_MODEL), F32),
        scratch_shapes=[pltpu.VMEM((X_SLOTS, tm, D_MODEL), F32), pltpu.SemaphoreType.DMA((X_SLOTS,))],
        compiler_params=pltpu.CompilerParams(
            dimension_semantics=("arbitrary",), vmem_limit_bytes=VMEM_LIMIT),
        name="outproj",
    )(a, b, x2d, g, w_bf16)


def kernel(x, norm_pre_g, w_in, diff_lambda, diff_subln_g, w_out, norm_post_g, rel_bias):
    b, s, d = x.shape
    assert d == D_MODEL and s == NBLK * BLK
    assert norm_pre_g.shape[0] == 1, "single-layer block"
    x2d = x.reshape(b * s, d)
    qkg, kext, vext, kmean = _inproj(x2d, norm_pre_g[0:1], w_in[0])
    qkg4 = qkg.reshape(b, NBLK, BLK, QKG_WIDTH)
    kext4 = kext.reshape(b, NBLK, BLK, KEXT_WIDTH)
    vext4 = vext.reshape(b, NBLK, BLK, VEXT_WIDTH)
    a_out = _diff_attention(qkg4, vext4, rel_bias, diff_lambda[0], diff_subln_g[0:1])
    b_out = _moba_attention(qkg4, kext4, vext4, kmean, rel_bias)
    out = _outproj(a_out, b_out, x2d, norm_post_g[0:1], w_out[0].astype(BF16))
    return out.reshape(b, s, d)
```

```python
import math

import numpy as np
import jax
import jax.numpy as jnp
from jax import lax
from jax.experimental import pallas as pl
from jax.experimental.pallas import tpu as pltpu

F32 = jnp.float32
BF16 = jnp.bfloat16

D_MODEL = 1024
DIFF_HEADS = 4
DIFF_QK = 64
MOBA_HEADS = 8
MOBA_DIM = 64
MOBA_TOPK = 3
N_BUCKETS = 32
MAX_DISTANCE = 128
IN_WIDTH = 4096
GROUP = 512
BLK = 256
NBLK = 8
NSLOT = NBLK + 1
PAIR_GROUPS = ((0, 1), (2, 3))
GROUP_PAIRS = 2
LANES = 128
NORM_EPS = 1e-6
SUBLN_EPS = 1e-5
QK_SCALE = 0.125
LOG2E = math.log2(math.e)
MASK_BIG = 1e30
VMEM_LIMIT = 56 * 1024 * 1024

QKG_WIDTH = 5 * GROUP
KEXT_WIDTH = 2 * GROUP
VEXT_WIDTH = 4 * GROUP

_NT = (((1,), (1,)), ((), ()))


def _rel_bucket(n):
    max_exact = N_BUCKETS // 2
    nf = np.maximum(n, 1).astype(np.float32)
    ratio = np.log(nf / np.float32(max_exact)) / np.float32(math.log(MAX_DISTANCE / max_exact))
    large = max_exact + (ratio * np.float32(N_BUCKETS - max_exact)).astype(np.int32)
    large = np.minimum(large, N_BUCKETS - 1)
    return np.where(n < max_exact, n, large).astype(np.int32)


def _bucket_vectors():
    l = np.arange(2 * BLK)
    out = []
    for t in range(2):
        d = t * BLK + BLK - 1 - l
        out.append(np.where(d >= 0, _rel_bucket(np.maximum(d, 0)), -1).astype(np.int32))
    return np.stack(out)[:, None, :]


def _inproj_kernel(x_ref, g_ref, w_ref, qkg_ref, kext_ref, vext_ref, kmean_ref, wbf_scr):
    tm = x_ref.shape[0]

    @pl.when(pl.program_id(0) == 0)
    def _cast_weights():
        for j in range(IN_WIDTH // GROUP):
            wbf_scr[:, j * GROUP:(j + 1) * GROUP] = w_ref[:, j * GROUP:(j + 1) * GROUP].astype(BF16)

    x = x_ref[...]
    ms = jnp.mean(x * x, axis=-1, keepdims=True)
    h = (x * lax.rsqrt(ms + NORM_EPS) * g_ref[...]).astype(BF16)

    def project(j):
        return jnp.dot(h, wbf_scr[:, j * GROUP:(j + 1) * GROUP], preferred_element_type=F32)

    qkg_ref[:, 0 * GROUP:1 * GROUP] = (project(0) * (QK_SCALE * LOG2E)).astype(BF16)
    qkg_ref[:, 1 * GROUP:2 * GROUP] = project(1).astype(BF16)
    qkg_ref[:, 2 * GROUP:3 * GROUP] = project(3).astype(BF16)
    qkg_ref[:, 3 * GROUP:4 * GROUP] = (project(4) * (QK_SCALE * LOG2E)).astype(BF16)
    qkg_ref[:, 4 * GROUP:5 * GROUP] = project(7).astype(BF16)

    ones = jnp.ones((tm, LANES), BF16)
    for base, j in ((0, 2), (GROUP // LANES, 6)):
        v = project(j).astype(BF16)
        for c in range(GROUP // LANES):
            vext_ref[:, (base + c) * 2 * LANES:(base + c) * 2 * LANES + LANES] = v[:, c * LANES:(c + 1) * LANES]
            vext_ref[:, (base + c) * 2 * LANES + LANES:(base + c + 1) * 2 * LANES] = ones

    mk = project(5)
    lane = lax.broadcasted_iota(jnp.int32, (BLK, LANES), 1)
    blocks_per_tile = tm // BLK
    first_block = (pl.program_id(0) * blocks_per_tile) % NBLK
    for r in range(blocks_per_tile):
        blk = mk[r * BLK:(r + 1) * BLK]
        kmean_ref[0, r:r + 1, :] = jnp.sum(blk, axis=0, keepdims=True) * (1.0 / BLK)
        n = first_block + r
        onehot = jnp.where((lane == n) | (lane == NBLK + n), 1.0, 0.0).astype(BF16)
        for p in range(GROUP // LANES):
            kext_ref[r * BLK:(r + 1) * BLK, p * 2 * LANES:p * 2 * LANES + LANES] = (
                blk[:, p * LANES:(p + 1) * LANES].astype(BF16))
            kext_ref[r * BLK:(r + 1) * BLK, p * 2 * LANES + LANES:(p + 1) * 2 * LANES] = onehot


def _inproj(x2d, g, w_f32, tm=2 * BLK):
    m = x2d.shape[0]
    row = lambda i: (i, 0)
    return pl.pallas_call(
        _inproj_kernel,
        grid=(m // tm,),
        in_specs=[
            pl.BlockSpec((tm, D_MODEL), row),
            pl.BlockSpec((1, D_MODEL), lambda i: (0, 0)),
            pl.BlockSpec((D_MODEL, IN_WIDTH), lambda i: (0, 0), pipeline_mode=pl.Buffered(1)),
        ],
        out_specs=[
            pl.BlockSpec((tm, QKG_WIDTH), row),
            pl.BlockSpec((tm, KEXT_WIDTH), row),
            pl.BlockSpec((tm, VEXT_WIDTH), row),
            pl.BlockSpec((1, tm // BLK, GROUP), lambda i: (i, 0, 0)),
        ],
        out_shape=[
            jax.ShapeDtypeStruct((m, QKG_WIDTH), BF16),
            jax.ShapeDtypeStruct((m, KEXT_WIDTH), BF16),
            jax.ShapeDtypeStruct((m, VEXT_WIDTH), BF16),
            jax.ShapeDtypeStruct((m // tm, tm // BLK, GROUP), F32),
        ],
        scratch_shapes=[pltpu.VMEM((D_MODEL, IN_WIDTH), BF16)],
        compiler_params=pltpu.CompilerParams(
            dimension_semantics=("arbitrary",), vmem_limit_bytes=VMEM_LIMIT),
        name="inproj",
    )(x2d, g, w_f32)


def _bias_kernel(tab_ref, bkt_ref, out_ref):
    g = pl.program_id(0)
    for half in range(2):
        head = jnp.where(g < DIFF_HEADS, g, 2 * g - DIFF_HEADS + half)
        for t in range(2):
            bk = bkt_ref[t]
            vec = jnp.full(bk.shape, -jnp.inf, F32)
            for b in range(N_BUCKETS):
                vec = jnp.where(bk == b, tab_ref[b, head] * LOG2E, vec)
            rows = jnp.broadcast_to(vec, (BLK, 2 * BLK))
            rolled = pltpu.roll(rows, 1, 1, stride=1, stride_axis=0)
            out_ref[0, t, half * BLK:(half + 1) * BLK, :] = rolled[:, BLK:]
        out_ref[0, 2, half * BLK:(half + 1) * BLK, :] = jnp.full((BLK, BLK), tab_ref[N_BUCKETS - 1, head] * LOG2E, F32)


def _bias_tiles(rel_bias):
    bkt = jnp.asarray(_bucket_vectors())
    ngroups = DIFF_HEADS + MOBA_HEADS // 2
    return pl.pallas_call(
        _bias_kernel,
        grid=(ngroups,),
        in_specs=[
            pl.BlockSpec(memory_space=pltpu.SMEM),
            pl.BlockSpec((2, 1, 2 * BLK), lambda g: (0, 0, 0)),
        ],
        out_specs=pl.BlockSpec((1, 3, 2 * BLK, BLK), lambda g: (g, 0, 0, 0)),
        out_shape=jax.ShapeDtypeStruct((ngroups, 3, 2 * BLK, BLK), F32),
        compiler_params=pltpu.CompilerParams(dimension_semantics=("arbitrary",)),
        name="bias_tiles",
    )(rel_bias, bkt)


def _slot_plan(pa, k):
    if k <= pa:
        return 0, k, min(pa - k, 2)
    kb = k - pa - 1
    return 1, kb, min(NBLK - 1 - pa - kb, 2)


def _emit_streams(scr, finish, scores=None, weighted=None):
    qe_scr, s_scr, m_scr = scr
    for j in range(GROUP_PAIRS):
        m_run = [None, None]
        acc_run = [None, None]
        for k in range(NSLOT):
            if scores is not None:
                g, k_at, bias_ref = scores
                pa = PAIR_GROUPS[g][j]
                sel, kb, idx = _slot_plan(pa, k)
                s = lax.dot_general(qe_scr[g, j, sel], k_at(kb), _NT, preferred_element_type=F32)
                s = s + bias_ref[0, idx]
                s_scr[g, j, k] = s
                t = jnp.maximum(s[:, :LANES], s[:, LANES:])
                m_run[sel] = t if m_run[sel] is None else jnp.maximum(m_run[sel], t)
                if kb == (pa, NBLK - 1 - pa)[sel]:
                    m_row = jnp.max(m_run[sel], axis=-1, keepdims=True)
                    m_scr[g, j, sel] = jnp.broadcast_to(m_row, (2 * BLK, LANES))
            if weighted is not None:
                g, v_at = weighted
                pa = PAIR_GROUPS[g][j]
                sel, kb, _ = _slot_plan(pa, k)
                mb = m_scr[g, j, sel]
                s = s_scr[g, j, k]
                p = jnp.concatenate([jnp.exp2(s[:, :LANES] - mb), jnp.exp2(s[:, LANES:] - mb)], axis=1).astype(BF16)
                pv = jnp.dot(p, v_at(kb), preferred_element_type=F32)
                acc_run[sel] = pv if acc_run[sel] is None else acc_run[sel] + pv
                if kb == (pa, NBLK - 1 - pa)[sel]:
                    acc = acc_run[sel]
                    finish((pa, NBLK - 1 - pa)[sel], acc[:, :LANES] / acc[:, LANES:])


def _region(trips, fn):
    def body(i, carry):
        fn()
        return carry
    lax.fori_loop(0, trips, body, 0)


def _pipelined_step(build_cur, build_nxt, k_cur, k_nxt, v_cur, bias_cur, bias_nxt, finish, scr):
    first = (pl.program_id(0) == 0) & (pl.program_id(1) == 0)
    one = jnp.minimum(pl.program_id(0) + 1, 1)

    def prologue():
        build_cur(0)
        build_cur(1)

    def prologue_scores():
        _emit_streams(scr, finish, scores=(0, k_cur, bias_cur))

    def region1():
        build_nxt(0)
        _emit_streams(scr, finish, scores=(1, k_cur, bias_cur), weighted=(0, v_cur))

    def region2():
        build_nxt(1)
        _emit_streams(scr, finish, scores=(0, k_nxt, bias_nxt), weighted=(1, v_cur))

    _region(jnp.where(first, 1, 0), prologue)
    _region(jnp.where(first, 1, 0), prologue_scores)
    _region(one, region1)
    _region(one, region2)


def _silu(g):
    return g / (1.0 + jnp.exp(-g))


def _core_scratch(kc):
    ng = len(PAIR_GROUPS)
    return [
        pltpu.VMEM((ng, GROUP_PAIRS, 2, 2 * BLK, kc), BF16),
        pltpu.VMEM((ng, GROUP_PAIRS, NSLOT, 2 * BLK, BLK), F32),
        pltpu.VMEM((ng, GROUP_PAIRS, 2, 2 * BLK, LANES), F32),
    ]


def _step_maps(nb, ncol):
    def cur(bi, h):
        return bi, h

    def nxt(bi, h):
        flat = jnp.minimum(bi * ncol + h + 1, nb * ncol - 1)
        return flat // ncol, flat % ncol

    return cur, nxt


def _seq_spec(width, which, col0):
    def index_map(bi, h):
        b2, h2 = which(bi, h)
        return b2, 0, 0, col0 + h2
    return pl.BlockSpec((1, NBLK, BLK, width), index_map)


def _bias_spec(which, bias0):
    def index_map(bi, h):
        _, h2 = which(bi, h)
        return bias0 + h2, 0, 0, 0
    return pl.BlockSpec((1, 3, 2 * BLK, BLK), index_map)


def _diff_kernel(q_cur, q_nxt, k_cur, k_nxt, v_cur, g_cur, bias_cur, bias_nxt, lam_ref, sg_ref, o_ref, *scr):
    qe_scr = scr[0]
    lane = lax.broadcasted_iota(jnp.int32, (BLK, LANES), 1)

    def build(q_ref, g):
        for j, pa in enumerate(PAIR_GROUPS[g]):
            for t, qt in enumerate((pa, NBLK - 1 - pa)):
                q = q_ref[0, qt].astype(F32)
                qe_scr[g, j, t, :BLK, :] = jnp.where(lane < DIFF_QK, q, 0.0).astype(BF16)
                qe_scr[g, j, t, BLK:, :] = jnp.where(lane >= DIFF_QK, q, 0.0).astype(BF16)

    def finish(qt, o):
        lp = lam_ref[...]
        lambda_init = 0.8 - 0.6 * math.exp(-0.3 * 0)
        lam = (jnp.exp(jnp.sum(lp[0:1] * lp[1:2], axis=-1, keepdims=True))
               - jnp.exp(jnp.sum(lp[2:3] * lp[3:4], axis=-1, keepdims=True)) + lambda_init)
        out = o[:BLK] - lam * o[BLK:]
        ms = jnp.mean(out * out, axis=-1, keepdims=True)
        out = out * lax.rsqrt(ms + SUBLN_EPS) * sg_ref[...] * (1.0 - lambda_init)
        out = out * _silu(g_cur[0, qt].astype(F32))
        o_ref[qt * BLK:(qt + 1) * BLK, :] = out.astype(BF16)

    _pipelined_step(lambda g: build(q_cur, g), lambda g: build(q_nxt, g),
                    lambda kb: k_cur[0, kb], lambda kb: k_nxt[0, kb], lambda kb: v_cur[0, kb],
                    bias_cur, bias_nxt, finish, scr)


def _diff_attention(qkg4, vext4, bias, diff_lambda, subln_g):
    nb = qkg4.shape[0]
    grp = GROUP // LANES
    cur, nxt = _step_maps(nb, DIFF_HEADS)
    const = lambda bi, h: (0, 0)
    return pl.pallas_call(
        _diff_kernel,
        grid=(nb, DIFF_HEADS),
        in_specs=[_seq_spec(LANES, cur, 0), _seq_spec(LANES, nxt, 0),
                  _seq_spec(LANES, cur, grp), _seq_spec(LANES, nxt, grp),
                  _seq_spec(2 * LANES, cur, 0), _seq_spec(LANES, cur, 2 * grp),
                  _bias_spec(cur, 0), _bias_spec(nxt, 0),
                  pl.BlockSpec((4, DIFF_QK), const), pl.BlockSpec((1, LANES), const)],
        out_specs=pl.BlockSpec((NBLK * BLK, LANES), lambda bi, h: (bi, h)),
        out_shape=jax.ShapeDtypeStruct((nb * NBLK * BLK, DIFF_HEADS * LANES), BF16),
        scratch_shapes=_core_scratch(LANES),
        compiler_params=pltpu.CompilerParams(
            dimension_semantics=("arbitrary", "arbitrary"), vmem_limit_bytes=VMEM_LIMIT),
        name="diff_attention",
    )(qkg4, qkg4, qkg4, qkg4, vext4, qkg4, bias, bias, diff_lambda, subln_g)


def _moba_kernel(q_cur, q_nxt, k_cur, k_nxt, v_cur, g_cur, bias_cur, bias_nxt, kmean_cur, kmean_nxt, o_ref,
                 wg_scr, *scr):
    qe_scr = scr[0]
    lane = lax.broadcasted_iota(jnp.int32, (BLK, LANES), 1)
    n_iota = lax.broadcasted_iota(jnp.int32, (NBLK, BLK), 0)

    def gate_operand(kmean_ref):
        wrow = lax.broadcasted_iota(jnp.int32, (LANES, LANES), 0)
        wlane = lax.broadcasted_iota(jnp.int32, (LANES, LANES), 1)
        wg = jnp.zeros((LANES, LANES), F32)
        per_tile = kmean_ref.shape[1]
        for n in range(NBLK):
            kmean = kmean_ref[n // per_tile, n % per_tile:n % per_tile + 1, :]
            hi = kmean.astype(BF16).astype(F32)
            mid = (kmean - hi).astype(BF16).astype(F32)
            lo = (kmean - hi - mid).astype(BF16).astype(F32)
            for piece, val in enumerate((hi, mid, lo)):
                for hh in range(2):
                    r = piece * 2 * NBLK + hh * NBLK + n
                    head_lanes = (wlane >= hh * MOBA_DIM) & (wlane < (hh + 1) * MOBA_DIM)
                    wg = jnp.where((wrow == r) & head_lanes, val, wg)
        return wg.astype(BF16)

    def build(q_ref, wg, g):
        for j, pa in enumerate(PAIR_GROUPS[g]):
            for t, qt in enumerate((pa, NBLK - 1 - pa)):
                q = q_ref[0, qt]
                qf = q.astype(F32)
                if qt == 0:
                    mask = jnp.zeros((BLK, LANES), F32)
                else:
                    g_t = lax.dot_general(wg, q, _NT, preferred_element_type=F32)
                    g_t = g_t[0:2 * NBLK] + g_t[2 * NBLK:4 * NBLK] + g_t[4 * NBLK:6 * NBLK]
                    mask_rows = []
                    for hh in range(2):
                        g8 = g_t[hh * NBLK:(hh + 1) * NBLK]
                        cnt = jnp.zeros((NBLK, BLK), jnp.int32)
                        for i in range(qt):
                            gi = g8[i:i + 1]
                            beats = (gi > g8) | ((gi == g8) & (i < n_iota))
                            cnt = cnt + jnp.where(beats, 1, 0)
                        keep = ((n_iota < qt) & (cnt < MOBA_TOPK)) | (n_iota == qt)
                        mask_rows.append(jnp.where(keep, 0.0, -MASK_BIG))
                    mask_t = jnp.concatenate(mask_rows + [jnp.zeros((LANES - 2 * NBLK, BLK), F32)], axis=0)
                    mask = mask_t.T
                for hh in range(2):
                    head_lanes = (lane >= hh * MOBA_DIM) & (lane < (hh + 1) * MOBA_DIM)
                    mask_lanes = (lane >= hh * NBLK) & (lane < (hh + 1) * NBLK)
                    qe_scr[g, j, t, hh * BLK:(hh + 1) * BLK, :LANES] = jnp.where(head_lanes, qf, 0.0).astype(BF16)
                    qe_scr[g, j, t, hh * BLK:(hh + 1) * BLK, LANES:] = jnp.where(mask_lanes, mask, 0.0).astype(BF16)

    def build_cur(g):
        build(q_cur, gate_operand(kmean_cur), g)

    def build_nxt(g):
        if g == 0:
            wg = gate_operand(kmean_nxt)
            wg_scr[...] = wg
        else:
            wg = wg_scr[...]
        build(q_nxt, wg, g)

    def finish(qt, o):
        out = jnp.where(lane < MOBA_DIM, o[:BLK], o[BLK:])
        out = out * _silu(g_cur[0, qt].astype(F32))
        o_ref[qt * BLK:(qt + 1) * BLK, :] = out.astype(BF16)

    _pipelined_step(build_cur, build_nxt,
                    lambda kb: k_cur[0, kb], lambda kb: k_nxt[0, kb], lambda kb: v_cur[0, kb],
                    bias_cur, bias_nxt, finish, scr)


def _moba_attention(qkg4, kext4, vext4, kmean, bias):
    nb = qkg4.shape[0]
    grp = GROUP // LANES
    pairs = MOBA_HEADS // 2
    cur, nxt = _step_maps(nb, pairs)
    tiles = kmean.shape[0] // nb

    def kmean_spec(which):
        def index_map(bi, h):
            b2, h2 = which(bi, h)
            return b2, 0, h2
        return pl.BlockSpec((tiles, kmean.shape[1], LANES), index_map)

    return pl.pallas_call(
        _moba_kernel,
        grid=(nb, pairs),
        in_specs=[_seq_spec(LANES, cur, 3 * grp), _seq_spec(LANES, nxt, 3 * grp),
                  _seq_spec(2 * LANES, cur, 0), _seq_spec(2 * LANES, nxt, 0),
                  _seq_spec(2 * LANES, cur, grp), _seq_spec(LANES, cur, 4 * grp),
                  _bias_spec(cur, DIFF_HEADS), _bias_spec(nxt, DIFF_HEADS),
                  kmean_spec(cur), kmean_spec(nxt)],
        out_specs=pl.BlockSpec((NBLK * BLK, LANES), lambda bi, h: (bi, h)),
        out_shape=jax.ShapeDtypeStruct((nb * NBLK * BLK, pairs * LANES), BF16),
        scratch_shapes=[pltpu.VMEM((LANES, LANES), BF16)] + _core_scratch(2 * LANES),
        compiler_params=pltpu.CompilerParams(
            dimension_semantics=("arbitrary", "arbitrary"), vmem_limit_bytes=VMEM_LIMIT),
        name="moba_attention",
    )(qkg4, qkg4, kext4, kext4, vext4, qkg4, bias, bias, kmean, kmean)


X_SLOTS = 3


def _outproj_kernel(a_ref, b_ref, x_hbm, g_ref, w_ref, o_ref, x_buf, x_sem):
    i = pl.program_id(0)
    nsteps = pl.num_programs(0)
    tm, half = a_ref.shape

    def x_copy(step):
        slot = step % X_SLOTS
        return pltpu.make_async_copy(x_hbm.at[pl.ds(step * tm, tm), :], x_buf.at[slot], x_sem.at[slot])

    @pl.when(i == 0)
    def _start_first():
        x_copy(0).start()

        @pl.when(nsteps > 1)
        def _():
            x_copy(1).start()

    @pl.when(i + 2 < nsteps)
    def _start_ahead():
        x_copy(i + 2).start()

    y = jnp.dot(a_ref[...], w_ref[:half, :], preferred_element_type=F32)
    y = y + jnp.dot(b_ref[...], w_ref[half:, :], preferred_element_type=F32)
    ms = jnp.mean(y * y, axis=-1, keepdims=True)
    y = y * lax.rsqrt(ms + NORM_EPS) * g_ref[...]
    x_copy(i).wait()
    o_ref[...] = x_buf[i % X_SLOTS] + y


def _outproj(a, b, x2d, g, w_bf16, tm=1024):
    m = x2d.shape[0]
    half = a.shape[1]
    return pl.pallas_call(
        _outproj_kernel,
        grid=(m // tm,),
        in_specs=[
            pl.BlockSpec((tm, half), lambda i: (i, 0)),
            pl.BlockSpec((tm, half), lambda i: (i, 0)),
            pl.BlockSpec(memory_space=pl.ANY),
            pl.BlockSpec((1, D_MODEL), lambda i: (0, 0)),
            pl.BlockSpec((2 * half, D_MODEL), lambda i: (0, 0)),
        ],
        out_specs=pl.BlockSpec((tm, D_MODEL), lambda i: (i, 0)),
        out_shape=jax.ShapeDtypeStruct((m, D_MODEL), F32),
        scratch_shapes=[pltpu.VMEM((X_SLOTS, tm, D_MODEL), F32), pltpu.SemaphoreType.DMA((X_SLOTS,))],
        compiler_params=pltpu.CompilerParams(
            dimension_semantics=("arbitrary",), vmem_limit_bytes=VMEM_LIMIT),
        name="outproj",
    )(a, b, x2d, g, w_bf16)


def kernel(x, norm_pre_g, w_in, diff_lambda, diff_subln_g, w_out, norm_post_g, rel_bias):
    b, s, d = x.shape
    assert d == D_MODEL and s == NBLK * BLK
    assert norm_pre_g.shape[0] == 1, "single-layer block"
    x2d = x.reshape(b * s, d)
    qkg, kext, vext, kmean = _inproj(x2d, norm_pre_g[0:1], w_in[0])
    qkg4 = qkg.reshape(b, NBLK, BLK, QKG_WIDTH)
    kext4 = kext.reshape(b, NBLK, BLK, KEXT_WIDTH)
    vext4 = vext.reshape(b, NBLK, BLK, VEXT_WIDTH)
    bias = _bias_tiles(rel_bias)
    a_out = _diff_attention(qkg4, vext4, bias, diff_lambda[0], diff_subln_g[0:1])
    b_out = _moba_attention(qkg4, kext4, vext4, kmean, bias)
    out = _outproj(a_out, b_out, x2d, norm_post_g[0:1], w_out[0].astype(BF16))
    return out.reshape(b, s, d)
```

```python
import math

import numpy as np
import jax
import jax.numpy as jnp
from jax import lax
from jax.experimental import pallas as pl
from jax.experimental.pallas import tpu as pltpu

F32 = jnp.float32
BF16 = jnp.bfloat16

D_MODEL = 1024
DIFF_HEADS = 4
DIFF_QK = 64
MOBA_HEADS = 8
MOBA_DIM = 64
MOBA_TOPK = 3
N_BUCKETS = 32
MAX_DISTANCE = 128
IN_WIDTH = 4096
GROUP = 512
BLK = 256
NBLK = 8
NSLOT = NBLK + 1
PAIR_GROUPS = ((0, 1), (2, 3))
GROUP_PAIRS = 2
LANES = 128
NORM_EPS = 1e-6
SUBLN_EPS = 1e-5
QK_SCALE = 0.125
LOG2E = math.log2(math.e)
MASK_BIG = 1e30
VMEM_LIMIT = 56 * 1024 * 1024

QKG_WIDTH = 5 * GROUP
KEXT_WIDTH = 2 * GROUP
VEXT_WIDTH = 4 * GROUP

_NT = (((1,), (1,)), ((), ()))


def _rel_bucket(n):
    max_exact = N_BUCKETS // 2
    nf = np.maximum(n, 1).astype(np.float32)
    ratio = np.log(nf / np.float32(max_exact)) / np.float32(math.log(MAX_DISTANCE / max_exact))
    large = max_exact + (ratio * np.float32(N_BUCKETS - max_exact)).astype(np.int32)
    large = np.minimum(large, N_BUCKETS - 1)
    return np.where(n < max_exact, n, large).astype(np.int32)


def _bucket_vectors():
    l = np.arange(2 * BLK)
    out = []
    for t in range(2):
        d = t * BLK + BLK - 1 - l
        out.append(np.where(d >= 0, _rel_bucket(np.maximum(d, 0)), -1).astype(np.int32))
    return np.stack(out)[:, None, :]


N_BIAS_GROUPS = DIFF_HEADS + MOBA_HEADS // 2


def _bias_group(tab_ref, bkt_ref, out_ref, g):
    for half in range(2):
        head = jnp.where(g < DIFF_HEADS, g, 2 * g - DIFF_HEADS + half)
        for t in range(2):
            bk = bkt_ref[t]
            vec = jnp.full(bk.shape, -jnp.inf, F32)
            for b in range(N_BUCKETS):
                vec = jnp.where(bk == b, tab_ref[b, head] * LOG2E, vec)
            rows = jnp.broadcast_to(vec, (BLK, 2 * BLK))
            rolled = pltpu.roll(rows, 1, 1, stride=1, stride_axis=0)
            out_ref[0, t, half * BLK:(half + 1) * BLK, :] = rolled[:, BLK:]
        out_ref[0, 2, half * BLK:(half + 1) * BLK, :] = jnp.full((BLK, BLK), tab_ref[N_BUCKETS - 1, head] * LOG2E, F32)


def _inproj_kernel(tab_ref, bkt_ref, x_ref, g_ref, w_ref, qkg_ref, kext_ref, vext_ref, kmean_ref, bias_ref, wbf_scr):
    tm = x_ref.shape[0]

    @pl.when(pl.program_id(0) == 0)
    def _cast_weights():
        for j in range(IN_WIDTH // GROUP):
            wbf_scr[:, j * GROUP:(j + 1) * GROUP] = w_ref[:, j * GROUP:(j + 1) * GROUP].astype(BF16)

    x = x_ref[...]
    ms = jnp.mean(x * x, axis=-1, keepdims=True)
    h = (x * lax.rsqrt(ms + NORM_EPS) * g_ref[...]).astype(BF16)

    def project(j):
        return jnp.dot(h, wbf_scr[:, j * GROUP:(j + 1) * GROUP], preferred_element_type=F32)

    mk = project(5)
    lane = lax.broadcasted_iota(jnp.int32, (BLK, LANES), 1)
    blocks_per_tile = tm // BLK
    first_block = (pl.program_id(0) * blocks_per_tile) % NBLK
    for r in range(blocks_per_tile):
        blk = mk[r * BLK:(r + 1) * BLK]
        kmean_ref[0, r:r + 1, :] = jnp.sum(blk, axis=0, keepdims=True) * (1.0 / BLK)
        n = first_block + r
        onehot = jnp.where((lane == n) | (lane == NBLK + n), 1.0, 0.0).astype(BF16)
        for p in range(GROUP // LANES):
            kext_ref[r * BLK:(r + 1) * BLK, p * 2 * LANES:p * 2 * LANES + LANES] = (
                blk[:, p * LANES:(p + 1) * LANES].astype(BF16))
            kext_ref[r * BLK:(r + 1) * BLK, p * 2 * LANES + LANES:(p + 1) * 2 * LANES] = onehot

    ones = jnp.ones((tm, LANES), BF16)
    for base, j in ((0, 2), (GROUP // LANES, 6)):
        v = project(j).astype(BF16)
        for c in range(GROUP // LANES):
            vext_ref[:, (base + c) * 2 * LANES:(base + c) * 2 * LANES + LANES] = v[:, c * LANES:(c + 1) * LANES]
            vext_ref[:, (base + c) * 2 * LANES + LANES:(base + c + 1) * 2 * LANES] = ones

    qkg_ref[:, 0 * GROUP:1 * GROUP] = (project(0) * (QK_SCALE * LOG2E)).astype(BF16)
    qkg_ref[:, 1 * GROUP:2 * GROUP] = project(1).astype(BF16)
    qkg_ref[:, 2 * GROUP:3 * GROUP] = project(3).astype(BF16)
    qkg_ref[:, 3 * GROUP:4 * GROUP] = (project(4) * (QK_SCALE * LOG2E)).astype(BF16)
    qkg_ref[:, 4 * GROUP:5 * GROUP] = project(7).astype(BF16)

    _bias_group(tab_ref, bkt_ref, bias_ref, pl.program_id(0) // (pl.num_programs(0) // N_BIAS_GROUPS))


def _inproj(x2d, g, w_f32, rel_bias, tm=2 * BLK):
    m = x2d.shape[0]
    row = lambda i: (i, 0)
    bkt = jnp.asarray(_bucket_vectors())
    return pl.pallas_call(
        _inproj_kernel,
        grid=(m // tm,),
        in_specs=[
            pl.BlockSpec(memory_space=pltpu.SMEM),
            pl.BlockSpec(bkt.shape, lambda i: (0, 0, 0)),
            pl.BlockSpec((tm, D_MODEL), row),
            pl.BlockSpec((1, D_MODEL), lambda i: (0, 0)),
            pl.BlockSpec((D_MODEL, IN_WIDTH), lambda i: (0, 0), pipeline_mode=pl.Buffered(1)),
        ],
        out_specs=[
            pl.BlockSpec((tm, QKG_WIDTH), row),
            pl.BlockSpec((tm, KEXT_WIDTH), row),
            pl.BlockSpec((tm, VEXT_WIDTH), row),
            pl.BlockSpec((1, tm // BLK, GROUP), lambda i: (i, 0, 0)),
            pl.BlockSpec((1, 3, 2 * BLK, BLK), lambda i: (i // (m // tm // N_BIAS_GROUPS), 0, 0, 0)),
        ],
        out_shape=[
            jax.ShapeDtypeStruct((m, QKG_WIDTH), BF16),
            jax.ShapeDtypeStruct((m, KEXT_WIDTH), BF16),
            jax.ShapeDtypeStruct((m, VEXT_WIDTH), BF16),
            jax.ShapeDtypeStruct((m // tm, tm // BLK, GROUP), F32),
            jax.ShapeDtypeStruct((N_BIAS_GROUPS, 3, 2 * BLK, BLK), F32),
        ],
        scratch_shapes=[pltpu.VMEM((D_MODEL, IN_WIDTH), BF16)],
        compiler_params=pltpu.CompilerParams(
            dimension_semantics=("arbitrary",), vmem_limit_bytes=VMEM_LIMIT),
        name="inproj",
    )(rel_bias, bkt, x2d, g, w_f32)


def _slot_plan(pa, k):
    if k <= pa:
        return 0, k, min(pa - k, 2)
    kb = k - pa - 1
    return 1, kb, min(NBLK - 1 - pa - kb, 2)


def _emit_streams(scr, finish, scores=None, weighted=None):
    qe_scr, s_scr, m_scr = scr
    for j in range(GROUP_PAIRS):
        m_run = [None, None]
        acc_run = [None, None]
        for k in range(NSLOT):
            if scores is not None:
                g, k_at, bias_ref = scores
                pa = PAIR_GROUPS[g][j]
                sel, kb, idx = _slot_plan(pa, k)
                s = lax.dot_general(qe_scr[g, j, sel], k_at(kb), _NT, preferred_element_type=F32)
                s = s + bias_ref[0, idx]
                s_scr[g, j, k] = s
                t = jnp.maximum(s[:, :LANES], s[:, LANES:])
                m_run[sel] = t if m_run[sel] is None else jnp.maximum(m_run[sel], t)
                if kb == (pa, NBLK - 1 - pa)[sel]:
                    m_row = jnp.max(m_run[sel], axis=-1, keepdims=True)
                    m_scr[g, j, sel] = jnp.broadcast_to(m_row, (2 * BLK, LANES))
            if weighted is not None:
                g, v_at = weighted
                pa = PAIR_GROUPS[g][j]
                sel, kb, _ = _slot_plan(pa, k)
                mb = m_scr[g, j, sel]
                s = s_scr[g, j, k]
                p = jnp.concatenate([jnp.exp2(s[:, :LANES] - mb), jnp.exp2(s[:, LANES:] - mb)], axis=1).astype(BF16)
                pv = jnp.dot(p, v_at(kb), preferred_element_type=F32)
                acc_run[sel] = pv if acc_run[sel] is None else acc_run[sel] + pv
                if kb == (pa, NBLK - 1 - pa)[sel]:
                    acc = acc_run[sel]
                    finish((pa, NBLK - 1 - pa)[sel], acc[:, :LANES] / acc[:, LANES:])


def _region(trips, fn):
    def body(i, carry):
        fn()
        return carry
    lax.fori_loop(0, trips, body, 0)


def _pipelined_step(build_cur, build_nxt, k_cur, k_nxt, v_cur, bias_cur, bias_nxt, finish, scr):
    first = (pl.program_id(0) == 0) & (pl.program_id(1) == 0)
    one = jnp.minimum(pl.program_id(0) + 1, 1)

    def prologue():
        build_cur(0)
        build_cur(1)

    def prologue_scores():
        _emit_streams(scr, finish, scores=(0, k_cur, bias_cur))

    def region1():
        build_nxt(0)
        _emit_streams(scr, finish, scores=(1, k_cur, bias_cur), weighted=(0, v_cur))

    def region2():
        build_nxt(1)
        _emit_streams(scr, finish, scores=(0, k_nxt, bias_nxt), weighted=(1, v_cur))

    _region(jnp.where(first, 1, 0), prologue)
    _region(jnp.where(first, 1, 0), prologue_scores)
    _region(one, region1)
    _region(one, region2)


def _silu(g):
    return g / (1.0 + jnp.exp(-g))


def _core_scratch(kc):
    ng = len(PAIR_GROUPS)
    return [
        pltpu.VMEM((ng, GROUP_PAIRS, 2, 2 * BLK, kc), BF16),
        pltpu.VMEM((ng, GROUP_PAIRS, NSLOT, 2 * BLK, BLK), F32),
        pltpu.VMEM((ng, GROUP_PAIRS, 2, 2 * BLK, LANES), F32),
    ]


def _step_maps(nb, ncol):
    def cur(bi, h):
        return bi, h

    def nxt(bi, h):
        flat = jnp.minimum(bi * ncol + h + 1, nb * ncol - 1)
        return flat // ncol, flat % ncol

    return cur, nxt


def _seq_spec(width, which, col0):
    def index_map(bi, h):
        b2, h2 = which(bi, h)
        return b2, 0, 0, col0 + h2
    return pl.BlockSpec((1, NBLK, BLK, width), index_map)


def _bias_spec(which, bias0):
    def index_map(bi, h):
        _, h2 = which(bi, h)
        return bias0 + h2, 0, 0, 0
    return pl.BlockSpec((1, 3, 2 * BLK, BLK), index_map)


def _diff_kernel(q_cur, q_nxt, k_cur, k_nxt, v_cur, g_cur, bias_cur, bias_nxt, lam_ref, sg_ref, o_ref, *scr):
    qe_scr = scr[0]
    lane = lax.broadcasted_iota(jnp.int32, (BLK, LANES), 1)

    def build(q_ref, g):
        for j, pa in enumerate(PAIR_GROUPS[g]):
            for t, qt in enumerate((pa, NBLK - 1 - pa)):
                q = q_ref[0, qt].astype(F32)
                qe_scr[g, j, t, :BLK, :] = jnp.where(lane < DIFF_QK, q, 0.0).astype(BF16)
                qe_scr[g, j, t, BLK:, :] = jnp.where(lane >= DIFF_QK, q, 0.0).astype(BF16)

    def finish(qt, o):
        lp = lam_ref[...]
        lambda_init = 0.8 - 0.6 * math.exp(-0.3 * 0)
        lam = (jnp.exp(jnp.sum(lp[0:1] * lp[1:2], axis=-1, keepdims=True))
               - jnp.exp(jnp.sum(lp[2:3] * lp[3:4], axis=-1, keepdims=True)) + lambda_init)
        out = o[:BLK] - lam * o[BLK:]
        ms = jnp.mean(out * out, axis=-1, keepdims=True)
        out = out * lax.rsqrt(ms + SUBLN_EPS) * sg_ref[...] * (1.0 - lambda_init)
        out = out * _silu(g_cur[0, qt].astype(F32))
        o_ref[qt * BLK:(qt + 1) * BLK, :] = out.astype(BF16)

    _pipelined_step(lambda g: build(q_cur, g), lambda g: build(q_nxt, g),
                    lambda kb: k_cur[0, kb], lambda kb: k_nxt[0, kb], lambda kb: v_cur[0, kb],
                    bias_cur, bias_nxt, finish, scr)


def _diff_attention(qkg4, vext4, bias, diff_lambda, subln_g):
    nb = qkg4.shape[0]
    grp = GROUP // LANES
    cur, nxt = _step_maps(nb, DIFF_HEADS)
    const = lambda bi, h: (0, 0)
    return pl.pallas_call(
        _diff_kernel,
        grid=(nb, DIFF_HEADS),
        in_specs=[_seq_spec(LANES, cur, 0), _seq_spec(LANES, nxt, 0),
                  _seq_spec(LANES, cur, grp), _seq_spec(LANES, nxt, grp),
                  _seq_spec(2 * LANES, cur, 0), _seq_spec(LANES, cur, 2 * grp),
                  _bias_spec(cur, 0), _bias_spec(nxt, 0),
                  pl.BlockSpec((4, DIFF_QK), const), pl.BlockSpec((1, LANES), const)],
        out_specs=pl.BlockSpec((NBLK * BLK, LANES), lambda bi, h: (bi, h)),
        out_shape=jax.ShapeDtypeStruct((nb * NBLK * BLK, DIFF_HEADS * LANES), BF16),
        scratch_shapes=_core_scratch(LANES),
        compiler_params=pltpu.CompilerParams(
            dimension_semantics=("arbitrary", "arbitrary"), vmem_limit_bytes=VMEM_LIMIT),
        name="diff_attention",
    )(qkg4, qkg4, qkg4, qkg4, vext4, qkg4, bias, bias, diff_lambda, subln_g)


def _moba_kernel(q_cur, q_nxt, k_cur, k_nxt, v_cur, g_cur, bias_cur, bias_nxt, kmean_cur, kmean_nxt, o_ref,
                 wg_scr, *scr):
    qe_scr = scr[0]
    lane = lax.broadcasted_iota(jnp.int32, (BLK, LANES), 1)
    n_iota = lax.broadcasted_iota(jnp.int32, (NBLK, BLK), 0)

    def gate_operand(kmean_ref):
        wrow = lax.broadcasted_iota(jnp.int32, (LANES, LANES), 0)
        wlane = lax.broadcasted_iota(jnp.int32, (LANES, LANES), 1)
        wg = jnp.zeros((LANES, LANES), F32)
        per_tile = kmean_ref.shape[1]
        for n in range(NBLK):
            kmean = kmean_ref[n // per_tile, n % per_tile:n % per_tile + 1, :]
            hi = kmean.astype(BF16).astype(F32)
            mid = (kmean - hi).astype(BF16).astype(F32)
            lo = (kmean - hi - mid).astype(BF16).astype(F32)
            for piece, val in enumerate((hi, mid, lo)):
                for hh in range(2):
                    r = piece * 2 * NBLK + hh * NBLK + n
                    head_lanes = (wlane >= hh * MOBA_DIM) & (wlane < (hh + 1) * MOBA_DIM)
                    wg = jnp.where((wrow == r) & head_lanes, val, wg)
        return wg.astype(BF16)

    def build(q_ref, wg, g):
        for j, pa in enumerate(PAIR_GROUPS[g]):
            for t, qt in enumerate((pa, NBLK - 1 - pa)):
                q = q_ref[0, qt]
                qf = q.astype(F32)
                if qt == 0:
                    mask = jnp.zeros((BLK, LANES), F32)
                else:
                    g_t = lax.dot_general(wg, q, _NT, preferred_element_type=F32)
                    g_t = g_t[0:2 * NBLK] + g_t[2 * NBLK:4 * NBLK] + g_t[4 * NBLK:6 * NBLK]
                    mask_rows = []
                    for hh in range(2):
                        g8 = g_t[hh * NBLK:(hh + 1) * NBLK]
                        cnt = jnp.zeros((NBLK, BLK), jnp.int32)
                        for i in range(qt):
                            gi = g8[i:i + 1]
                            beats = (gi > g8) | ((gi == g8) & (i < n_iota))
                            cnt = cnt + jnp.where(beats, 1, 0)
                        keep = ((n_iota < qt) & (cnt < MOBA_TOPK)) | (n_iota == qt)
                        mask_rows.append(jnp.where(keep, 0.0, -MASK_BIG))
                    mask_t = jnp.concatenate(mask_rows + [jnp.zeros((LANES - 2 * NBLK, BLK), F32)], axis=0)
                    mask = mask_t.T
                for hh in range(2):
                    head_lanes = (lane >= hh * MOBA_DIM) & (lane < (hh + 1) * MOBA_DIM)
                    mask_lanes = (lane >= hh * NBLK) & (lane < (hh + 1) * NBLK)
                    qe_scr[g, j, t, hh * BLK:(hh + 1) * BLK, :LANES] = jnp.where(head_lanes, qf, 0.0).astype(BF16)
                    qe_scr[g, j, t, hh * BLK:(hh + 1) * BLK, LANES:] = jnp.where(mask_lanes, mask, 0.0).astype(BF16)

    def build_cur(g):
        build(q_cur, gate_operand(kmean_cur), g)

    def build_nxt(g):
        if g == 0:
            wg = gate_operand(kmean_nxt)
            wg_scr[...] = wg
        else:
            wg = wg_scr[...]
        build(q_nxt, wg, g)

    def finish(qt, o):
        out = jnp.where(lane < MOBA_DIM, o[:BLK], o[BLK:])
        out = out * _silu(g_cur[0, qt].astype(F32))
        o_ref[qt * BLK:(qt + 1) * BLK, :] = out.astype(BF16)

    _pipelined_step(build_cur, build_nxt,
                    lambda kb: k_cur[0, kb], lambda kb: k_nxt[0, kb], lambda kb: v_cur[0, kb],
                    bias_cur, bias_nxt, finish, scr)


def _moba_attention(qkg4, kext4, vext4, kmean, bias):
    nb = qkg4.shape[0]
    grp = GROUP // LANES
    pairs = MOBA_HEADS // 2
    cur, nxt = _step_maps(nb, pairs)
    tiles = kmean.shape[0] // nb

    def kmean_spec(which):
        def index_map(bi, h):
            b2, h2 = which(bi, h)
            return b2, 0, h2
        return pl.BlockSpec((tiles, kmean.shape[1], LANES), index_map)

    return pl.pallas_call(
        _moba_kernel,
        grid=(nb, pairs),
        in_specs=[_seq_spec(LANES, cur, 3 * grp), _seq_spec(LANES, nxt, 3 * grp),
                  _seq_spec(2 * LANES, cur, 0), _seq_spec(2 * LANES, nxt, 0),
                  _seq_spec(2 * LANES, cur, grp), _seq_spec(LANES, cur, 4 * grp),
                  _bias_spec(cur, DIFF_HEADS), _bias_spec(nxt, DIFF_HEADS),
                  kmean_spec(cur), kmean_spec(nxt)],
        out_specs=pl.BlockSpec((NBLK * BLK, LANES), lambda bi, h: (bi, h)),
        out_shape=jax.ShapeDtypeStruct((nb * NBLK * BLK, pairs * LANES), BF16),
        scratch_shapes=[pltpu.VMEM((LANES, LANES), BF16)] + _core_scratch(2 * LANES),
        compiler_params=pltpu.CompilerParams(
            dimension_semantics=("arbitrary", "arbitrary"), vmem_limit_bytes=VMEM_LIMIT),
        name="moba_attention",
    )(qkg4, qkg4, kext4, kext4, vext4, qkg4, bias, bias, kmean, kmean)


X_SLOTS = 3


def _outproj_kernel(a_ref, b_ref, x_hbm, g_ref, w_ref, o_ref, x_buf, x_sem):
    i = pl.program_id(0)
    nsteps = pl.num_programs(0)
    tm, half = a_ref.shape

    def x_copy(step):
        slot = step % X_SLOTS
        return pltpu.make_async_copy(x_hbm.at[pl.ds(step * tm, tm), :], x_buf.at[slot], x_sem.at[slot])

    @pl.when(i == 0)
    def _start_first():
        x_copy(0).start()

        @pl.when(nsteps > 1)
        def _():
            x_copy(1).start()

    @pl.when(i + 2 < nsteps)
    def _start_ahead():
        x_copy(i + 2).start()

    y = jnp.dot(a_ref[...], w_ref[:half, :], preferred_element_type=F32)
    y = y + jnp.dot(b_ref[...], w_ref[half:, :], preferred_element_type=F32)
    ms = jnp.mean(y * y, axis=-1, keepdims=True)
    y = y * lax.rsqrt(ms + NORM_EPS) * g_ref[...]
    x_copy(i).wait()
    o_ref[...] = x_buf[i % X_SLOTS] + y


def _outproj(a, b, x2d, g, w_bf16, tm=1024):
    m = x2d.shape[0]
    half = a.shape[1]
    return pl.pallas_call(
        _outproj_kernel,
        grid=(m // tm,),
        in_specs=[
            pl.BlockSpec((tm, half), lambda i: (i, 0)),
            pl.BlockSpec((tm, half), lambda i: (i, 0)),
            pl.BlockSpec(memory_space=pl.ANY),
            pl.BlockSpec((1, D_MODEL), lambda i: (0, 0)),
            pl.BlockSpec((2 * half, D_MODEL), lambda i: (0, 0)),
        ],
        out_specs=pl.BlockSpec((tm, D_MODEL), lambda i: (i, 0)),
        out_shape=jax.ShapeDtypeStruct((m, D_MODEL), F32),
        scratch_shapes=[pltpu.VMEM((X_SLOTS, tm, D_MODEL), F32), pltpu.SemaphoreType.DMA((X_SLOTS,))],
        compiler_params=pltpu.CompilerParams(
            dimension_semantics=("arbitrary",), vmem_limit_bytes=VMEM_LIMIT),
        name="outproj",
    )(a, b, x2d, g, w_bf16)


def kernel(x, norm_pre_g, w_in, diff_lambda, diff_subln_g, w_out, norm_post_g, rel_bias):
    b, s, d = x.shape
    assert d == D_MODEL and s == NBLK * BLK
    assert norm_pre_g.shape[0] == 1, "single-layer block"
    x2d = x.reshape(b * s, d)
    qkg, kext, vext, kmean, bias = _inproj(x2d, norm_pre_g[0:1], w_in[0], rel_bias)
    qkg4 = qkg.reshape(b, NBLK, BLK, QKG_WIDTH)
    kext4 = kext.reshape(b, NBLK, BLK, KEXT_WIDTH)
    vext4 = vext.reshape(b, NBLK, BLK, VEXT_WIDTH)
    a_out = _diff_attention(qkg4, vext4, bias, diff_lambda[0], diff_subln_g[0:1])
    b_out = _moba_attention(qkg4, kext4, vext4, kmean, bias)
    out = _outproj(a_out, b_out, x2d, norm_post_g[0:1], w_out[0].astype(BF16))
    return out.reshape(b, s, d)
```

```python
import math

import numpy as np
import jax
import jax.numpy as jnp
from jax import lax
from jax.experimental import pallas as pl
from jax.experimental.pallas import tpu as pltpu

F32 = jnp.float32
BF16 = jnp.bfloat16

D_MODEL = 1024
DIFF_HEADS = 4
DIFF_QK = 64
MOBA_HEADS = 8
MOBA_DIM = 64
MOBA_TOPK = 3
N_BUCKETS = 32
MAX_DISTANCE = 128
IN_WIDTH = 4096
GROUP = 512
BLK = 256
NBLK = 8
NSLOT = NBLK + 1
PAIR_GROUPS = ((0, 1), (2, 3))
GROUP_PAIRS = 2
LANES = 128
NORM_EPS = 1e-6
SUBLN_EPS = 1e-5
QK_SCALE = 0.125
LOG2E = math.log2(math.e)
MASK_BIG = 1e30
VMEM_LIMIT = 56 * 1024 * 1024

QKG_WIDTH = 5 * GROUP
KEXT_WIDTH = 2 * GROUP
VEXT_WIDTH = 4 * GROUP

_NT = (((1,), (1,)), ((), ()))


def _rel_bucket(n):
    max_exact = N_BUCKETS // 2
    nf = np.maximum(n, 1).astype(np.float32)
    ratio = np.log(nf / np.float32(max_exact)) / np.float32(math.log(MAX_DISTANCE / max_exact))
    large = max_exact + (ratio * np.float32(N_BUCKETS - max_exact)).astype(np.int32)
    large = np.minimum(large, N_BUCKETS - 1)
    return np.where(n < max_exact, n, large).astype(np.int32)


def _bucket_vectors():
    l = np.arange(2 * BLK)
    out = []
    for t in range(2):
        d = t * BLK + BLK - 1 - l
        out.append(np.where(d >= 0, _rel_bucket(np.maximum(d, 0)), -1).astype(np.int32))
    return np.stack(out)[:, None, :]


N_BIAS_GROUPS = DIFF_HEADS + MOBA_HEADS // 2


def _bias_group(tab_ref, bkt_ref, out_ref, g):
    for half in range(2):
        head = jnp.where(g < DIFF_HEADS, g, 2 * g - DIFF_HEADS + half)
        for t in range(2):
            bk = bkt_ref[t]
            vec = jnp.full(bk.shape, -jnp.inf, F32)
            for b in range(N_BUCKETS):
                vec = jnp.where(bk == b, tab_ref[b, head] * LOG2E, vec)
            rows = jnp.broadcast_to(vec, (BLK, 2 * BLK))
            rolled = pltpu.roll(rows, 1, 1, stride=1, stride_axis=0)
            out_ref[0, t, half * BLK:(half + 1) * BLK, :] = rolled[:, BLK:]
        out_ref[0, 2, half * BLK:(half + 1) * BLK, :] = jnp.full((BLK, BLK), tab_ref[N_BUCKETS - 1, head] * LOG2E, F32)


def _inproj_kernel(tab_ref, bkt_ref, x_ref, g_ref, w_ref, qkg_ref, kext_ref, vext_ref, kmean_ref, bias_ref, wbf_scr):
    tm = x_ref.shape[0]

    @pl.when(pl.program_id(0) == 0)
    def _cast_weights():
        for j in range(IN_WIDTH // GROUP):
            wbf_scr[:, j * GROUP:(j + 1) * GROUP] = w_ref[:, j * GROUP:(j + 1) * GROUP].astype(BF16)

    x = x_ref[...]
    ms = jnp.mean(x * x, axis=-1, keepdims=True)
    h = (x * lax.rsqrt(ms + NORM_EPS) * g_ref[...]).astype(BF16)

    def project(j):
        return jnp.dot(h, wbf_scr[:, j * GROUP:(j + 1) * GROUP], preferred_element_type=F32)

    mk = project(5)
    lane = lax.broadcasted_iota(jnp.int32, (BLK, LANES), 1)
    blocks_per_tile = tm // BLK
    first_block = (pl.program_id(0) * blocks_per_tile) % NBLK
    for r in range(blocks_per_tile):
        blk = mk[r * BLK:(r + 1) * BLK]
        kmean_ref[0, r:r + 1, :] = jnp.sum(blk, axis=0, keepdims=True) * (1.0 / BLK)
        n = first_block + r
        onehot = jnp.where((lane == n) | (lane == NBLK + n), 1.0, 0.0).astype(BF16)
        for p in range(GROUP // LANES):
            kext_ref[r * BLK:(r + 1) * BLK, p * 2 * LANES:p * 2 * LANES + LANES] = (
                blk[:, p * LANES:(p + 1) * LANES].astype(BF16))
            kext_ref[r * BLK:(r + 1) * BLK, p * 2 * LANES + LANES:(p + 1) * 2 * LANES] = onehot

    ones = jnp.ones((tm, LANES), BF16)
    for base, j in ((0, 2), (GROUP // LANES, 6)):
        v = project(j).astype(BF16)
        for c in range(GROUP // LANES):
            vext_ref[:, (base + c) * 2 * LANES:(base + c) * 2 * LANES + LANES] = v[:, c * LANES:(c + 1) * LANES]
            vext_ref[:, (base + c) * 2 * LANES + LANES:(base + c + 1) * 2 * LANES] = ones

    qkg_ref[:, 0 * GROUP:1 * GROUP] = (project(0) * (QK_SCALE * LOG2E)).astype(BF16)
    qkg_ref[:, 1 * GROUP:2 * GROUP] = project(1).astype(BF16)
    qkg_ref[:, 2 * GROUP:3 * GROUP] = project(3).astype(BF16)
    qkg_ref[:, 3 * GROUP:4 * GROUP] = (project(4) * (QK_SCALE * LOG2E)).astype(BF16)
    qkg_ref[:, 4 * GROUP:5 * GROUP] = project(7).astype(BF16)

    _bias_group(tab_ref, bkt_ref, bias_ref, pl.program_id(0) // (pl.num_programs(0) // N_BIAS_GROUPS))


def _inproj(x2d, g, w_f32, rel_bias, tm=2 * BLK):
    m = x2d.shape[0]
    row = lambda i: (i, 0)
    bkt = jnp.asarray(_bucket_vectors())
    return pl.pallas_call(
        _inproj_kernel,
        grid=(m // tm,),
        in_specs=[
            pl.BlockSpec(memory_space=pltpu.SMEM),
            pl.BlockSpec(bkt.shape, lambda i: (0, 0, 0)),
            pl.BlockSpec((tm, D_MODEL), row),
            pl.BlockSpec((1, D_MODEL), lambda i: (0, 0)),
            pl.BlockSpec((D_MODEL, IN_WIDTH), lambda i: (0, 0), pipeline_mode=pl.Buffered(1)),
        ],
        out_specs=[
            pl.BlockSpec((tm, QKG_WIDTH), row),
            pl.BlockSpec((tm, KEXT_WIDTH), row),
            pl.BlockSpec((tm, VEXT_WIDTH), row),
            pl.BlockSpec((1, tm // BLK, GROUP), lambda i: (i, 0, 0)),
            pl.BlockSpec((1, 3, 2 * BLK, BLK), lambda i: (i // (m // tm // N_BIAS_GROUPS), 0, 0, 0)),
        ],
        out_shape=[
            jax.ShapeDtypeStruct((m, QKG_WIDTH), BF16),
            jax.ShapeDtypeStruct((m, KEXT_WIDTH), BF16),
            jax.ShapeDtypeStruct((m, VEXT_WIDTH), BF16),
            jax.ShapeDtypeStruct((m // tm, tm // BLK, GROUP), F32),
            jax.ShapeDtypeStruct((N_BIAS_GROUPS, 3, 2 * BLK, BLK), F32),
        ],
        scratch_shapes=[pltpu.VMEM((D_MODEL, IN_WIDTH), BF16)],
        compiler_params=pltpu.CompilerParams(
            dimension_semantics=("arbitrary",), vmem_limit_bytes=VMEM_LIMIT),
        name="inproj",
    )(rel_bias, bkt, x2d, g, w_f32)


def _slot_plan(pa, k):
    if k <= pa:
        return 0, k, min(pa - k, 2)
    kb = k - pa - 1
    return 1, kb, min(NBLK - 1 - pa - kb, 2)


def _emit_streams(scr, finish, scores=None, weighted=None):
    qe_scr, s_scr, m_scr = scr
    for j in range(GROUP_PAIRS):
        m_run = [None, None]
        acc_run = [None, None]
        for k in range(NSLOT):
            if scores is not None:
                g, k_at, bias_ref = scores
                pa = PAIR_GROUPS[g][j]
                sel, kb, idx = _slot_plan(pa, k)
                s = lax.dot_general(qe_scr[g, j, sel], k_at(kb), _NT, preferred_element_type=F32)
                s = s + bias_ref[0, idx]
                s_scr[g, j, k] = s
                t = jnp.maximum(s[:, :LANES], s[:, LANES:])
                m_run[sel] = t if m_run[sel] is None else jnp.maximum(m_run[sel], t)
                if kb == (pa, NBLK - 1 - pa)[sel]:
                    m_row = jnp.max(m_run[sel], axis=-1, keepdims=True)
                    m_scr[g, j, sel] = jnp.broadcast_to(m_row, (2 * BLK, LANES))
            if weighted is not None:
                g, v_at = weighted
                pa = PAIR_GROUPS[g][j]
                sel, kb, _ = _slot_plan(pa, k)
                mb = m_scr[g, j, sel]
                s = s_scr[g, j, k]
                p = jnp.concatenate([jnp.exp2(s[:, :LANES] - mb), jnp.exp2(s[:, LANES:] - mb)], axis=1).astype(BF16)
                pv = jnp.dot(p, v_at(kb), preferred_element_type=F32)
                acc_run[sel] = pv if acc_run[sel] is None else acc_run[sel] + pv
                if kb == (pa, NBLK - 1 - pa)[sel]:
                    acc = acc_run[sel]
                    finish((pa, NBLK - 1 - pa)[sel], acc[:, :LANES] / acc[:, LANES:])


def _region(trips, fn):
    def body(i, carry):
        fn()
        return carry
    lax.fori_loop(0, trips, body, 0)


def _pipelined_step(first, heads, scr):
    one = jnp.minimum(pl.program_id(0) + 1, 1)
    h0 = heads[0]

    def prologue():
        h0.build_cur(0)
        h0.build_cur(1)

    def prologue_scores():
        _emit_streams(scr, h0.finish, scores=(0, h0.k_cur, h0.bias_cur))

    _region(jnp.where(first, 1, 0), prologue)
    _region(jnp.where(first, 1, 0), prologue_scores)
    for hd in heads:
        def region1(hd=hd):
            hd.build_nxt(0)
            _emit_streams(scr, hd.finish, scores=(1, hd.k_cur, hd.bias_cur), weighted=(0, hd.v_cur))

        def region2(hd=hd):
            hd.build_nxt(1)
            _emit_streams(scr, hd.finish, scores=(0, hd.k_nxt, hd.bias_nxt), weighted=(1, hd.v_cur))

        _region(one, region1)
        _region(one, region2)


def _silu(g):
    return g / (1.0 + jnp.exp(-g))


def _core_scratch(kc):
    ng = len(PAIR_GROUPS)
    return [
        pltpu.VMEM((ng, GROUP_PAIRS, 2, 2 * BLK, kc), BF16),
        pltpu.VMEM((ng, GROUP_PAIRS, NSLOT, 2 * BLK, BLK), F32),
        pltpu.VMEM((ng, GROUP_PAIRS, 2, 2 * BLK, LANES), F32),
    ]


HEADS_PER_STEP = 2


class _Head:
    def __init__(self, **kw):
        self.__dict__.update(kw)


def _lanes(u, width=LANES):
    return slice(u * width, (u + 1) * width)


def _head_maps(nb, ncol):
    def cur(bi, hp):
        return bi, hp

    def nxt(bi, hp):
        flat = jnp.minimum((bi * (ncol // HEADS_PER_STEP) + hp + 1) * HEADS_PER_STEP, nb * ncol - 1)
        return flat // ncol, flat % ncol

    return cur, nxt


def _seq_spec(width, which, col0):
    def index_map(bi, hp):
        b2, h2 = which(bi, hp)
        return b2, 0, 0, col0 + h2
    return pl.BlockSpec((1, NBLK, BLK, width), index_map)


def _bias_spec(which, bias0, count):
    def index_map(bi, hp):
        _, h2 = which(bi, hp)
        return bias0 + h2, 0, 0, 0
    return pl.BlockSpec((count, 3, 2 * BLK, BLK), index_map)


def _diff_kernel(q2, q_nx, k2, k_nx, v2, g2, bias2, bias_nx, lam_ref, sg_ref, o_ref, *scr):
    qe_scr = scr[0]
    lane = lax.broadcasted_iota(jnp.int32, (BLK, LANES), 1)

    def build(q_at, g):
        for j, pa in enumerate(PAIR_GROUPS[g]):
            for t, qt in enumerate((pa, NBLK - 1 - pa)):
                q = q_at(qt).astype(F32)
                qe_scr[g, j, t, :BLK, :] = jnp.where(lane < DIFF_QK, q, 0.0).astype(BF16)
                qe_scr[g, j, t, BLK:, :] = jnp.where(lane >= DIFF_QK, q, 0.0).astype(BF16)

    def make_finish(u):
        def finish(qt, o):
            lp = lam_ref[...]
            lambda_init = 0.8 - 0.6 * math.exp(-0.3 * 0)
            lam = (jnp.exp(jnp.sum(lp[0:1] * lp[1:2], axis=-1, keepdims=True))
                   - jnp.exp(jnp.sum(lp[2:3] * lp[3:4], axis=-1, keepdims=True)) + lambda_init)
            out = o[:BLK] - lam * o[BLK:]
            ms = jnp.mean(out * out, axis=-1, keepdims=True)
            out = out * lax.rsqrt(ms + SUBLN_EPS) * sg_ref[...] * (1.0 - lambda_init)
            out = out * _silu(g2[0, qt][:, _lanes(u)].astype(F32))
            o_ref[qt * BLK:(qt + 1) * BLK, _lanes(u)] = out.astype(BF16)
        return finish

    def head(u):
        last = u == HEADS_PER_STEP - 1
        q_cur = lambda qt: q2[0, qt][:, _lanes(u)]
        q_nxt = (lambda qt: q_nx[0, qt]) if last else (lambda qt: q2[0, qt][:, _lanes(u + 1)])
        return _Head(
            build_cur=lambda g: build(q_cur, g), build_nxt=lambda g: build(q_nxt, g),
            k_cur=lambda kb: k2[0, kb][:, _lanes(u)],
            k_nxt=(lambda kb: k_nx[0, kb]) if last else (lambda kb: k2[0, kb][:, _lanes(u + 1)]),
            v_cur=lambda kb: v2[0, kb][:, _lanes(u, 2 * LANES)],
            bias_cur=bias2.at[u:u + 1], bias_nxt=bias_nx if last else bias2.at[u + 1:u + 2],
            finish=make_finish(u))

    first = (pl.program_id(0) == 0) & (pl.program_id(1) == 0)
    _pipelined_step(first, [head(u) for u in range(HEADS_PER_STEP)], scr)


def _diff_attention(qkg4, vext4, bias, diff_lambda, subln_g):
    nb = qkg4.shape[0]
    grp = GROUP // LANES
    hps = HEADS_PER_STEP
    cur, nxt = _head_maps(nb, DIFF_HEADS)
    const = lambda bi, hp: (0, 0)
    return pl.pallas_call(
        _diff_kernel,
        grid=(nb, DIFF_HEADS // hps),
        in_specs=[_seq_spec(hps * LANES, cur, 0), _seq_spec(LANES, nxt, 0),
                  _seq_spec(hps * LANES, cur, grp // hps), _seq_spec(LANES, nxt, grp),
                  _seq_spec(hps * 2 * LANES, cur, 0), _seq_spec(hps * LANES, cur, 2 * grp // hps),
                  _bias_spec(cur, 0, hps), _bias_spec(nxt, 0, 1),
                  pl.BlockSpec((4, DIFF_QK), const), pl.BlockSpec((1, LANES), const)],
        out_specs=pl.BlockSpec((NBLK * BLK, hps * LANES), lambda bi, hp: (bi, hp)),
        out_shape=jax.ShapeDtypeStruct((nb * NBLK * BLK, DIFF_HEADS * LANES), BF16),
        scratch_shapes=_core_scratch(LANES),
        compiler_params=pltpu.CompilerParams(
            dimension_semantics=("arbitrary", "arbitrary"), vmem_limit_bytes=VMEM_LIMIT),
        name="diff_attention",
    )(qkg4, qkg4, qkg4, qkg4, vext4, qkg4, bias, bias, diff_lambda, subln_g)


def _moba_kernel(q2, q_nx, k2, k_nx, v2, g2, bias2, bias_nx, kmean2, kmean_nx, o_ref, wg_scr, *scr):
    qe_scr = scr[0]
    lane = lax.broadcasted_iota(jnp.int32, (BLK, LANES), 1)
    n_iota = lax.broadcasted_iota(jnp.int32, (NBLK, BLK), 0)

    def gate_operand(kmean_at):
        wrow = lax.broadcasted_iota(jnp.int32, (LANES, LANES), 0)
        wlane = lax.broadcasted_iota(jnp.int32, (LANES, LANES), 1)
        wg = jnp.zeros((LANES, LANES), F32)
        for n in range(NBLK):
            kmean = kmean_at(n)
            hi = kmean.astype(BF16).astype(F32)
            mid = (kmean - hi).astype(BF16).astype(F32)
            lo = (kmean - hi - mid).astype(BF16).astype(F32)
            for piece, val in enumerate((hi, mid, lo)):
                for hh in range(2):
                    r = piece * 2 * NBLK + hh * NBLK + n
                    head_lanes = (wlane >= hh * MOBA_DIM) & (wlane < (hh + 1) * MOBA_DIM)
                    wg = jnp.where((wrow == r) & head_lanes, val, wg)
        return wg.astype(BF16)

    def build(q_at, wg, g):
        for j, pa in enumerate(PAIR_GROUPS[g]):
            for t, qt in enumerate((pa, NBLK - 1 - pa)):
                q = q_at(qt)
                qf = q.astype(F32)
                if qt == 0:
                    mask = jnp.zeros((BLK, LANES), F32)
                else:
                    g_t = lax.dot_general(wg, q, _NT, preferred_element_type=F32)
                    g_t = g_t[0:2 * NBLK] + g_t[2 * NBLK:4 * NBLK] + g_t[4 * NBLK:6 * NBLK]
                    mask_rows = []
                    for hh in range(2):
                        g8 = g_t[hh * NBLK:(hh + 1) * NBLK]
                        cnt = jnp.zeros((NBLK, BLK), jnp.int32)
                        for i in range(qt):
                            gi = g8[i:i + 1]
                            beats = (gi > g8) | ((gi == g8) & (i < n_iota))
                            cnt = cnt + jnp.where(beats, 1, 0)
                        keep = ((n_iota < qt) & (cnt < MOBA_TOPK)) | (n_iota == qt)
                        mask_rows.append(jnp.where(keep, 0.0, -MASK_BIG))
                    mask_t = jnp.concatenate(mask_rows + [jnp.zeros((LANES - 2 * NBLK, BLK), F32)], axis=0)
                    mask = mask_t.T
                for hh in range(2):
                    head_lanes = (lane >= hh * MOBA_DIM) & (lane < (hh + 1) * MOBA_DIM)
                    mask_lanes = (lane >= hh * NBLK) & (lane < (hh + 1) * NBLK)
                    qe_scr[g, j, t, hh * BLK:(hh + 1) * BLK, :LANES] = jnp.where(head_lanes, qf, 0.0).astype(BF16)
                    qe_scr[g, j, t, hh * BLK:(hh + 1) * BLK, LANES:] = jnp.where(mask_lanes, mask, 0.0).astype(BF16)

    per_tile = kmean2.shape[1]

    def kmean_getter(ref, lanes):
        return lambda n: ref[n // per_tile, n % per_tile:n % per_tile + 1, lanes]

    def make_finish(u):
        def finish(qt, o):
            out = jnp.where(lane < MOBA_DIM, o[:BLK], o[BLK:])
            out = out * _silu(g2[0, qt][:, _lanes(u)].astype(F32))
            o_ref[qt * BLK:(qt + 1) * BLK, _lanes(u)] = out.astype(BF16)
        return finish

    def head(u):
        last = u == HEADS_PER_STEP - 1
        q_cur = lambda qt: q2[0, qt][:, _lanes(u)]
        q_nxt = (lambda qt: q_nx[0, qt]) if last else (lambda qt: q2[0, qt][:, _lanes(u + 1)])
        km_cur = kmean_getter(kmean2, _lanes(u))
        km_nxt = kmean_getter(kmean_nx, slice(None)) if last else kmean_getter(kmean2, _lanes(u + 1))

        def build_nxt(g):
            if g == 0:
                wg = gate_operand(km_nxt)
                wg_scr[...] = wg
            else:
                wg = wg_scr[...]
            build(q_nxt, wg, g)

        return _Head(
            build_cur=lambda g: build(q_cur, gate_operand(km_cur), g), build_nxt=build_nxt,
            k_cur=lambda kb: k2[0, kb][:, _lanes(u, 2 * LANES)],
            k_nxt=(lambda kb: k_nx[0, kb]) if last else (lambda kb: k2[0, kb][:, _lanes(u + 1, 2 * LANES)]),
            v_cur=lambda kb: v2[0, kb][:, _lanes(u, 2 * LANES)],
            bias_cur=bias2.at[u:u + 1], bias_nxt=bias_nx if last else bias2.at[u + 1:u + 2],
            finish=make_finish(u))

    first = (pl.program_id(0) == 0) & (pl.program_id(1) == 0)
    _pipelined_step(first, [head(u) for u in range(HEADS_PER_STEP)], scr)


def _moba_attention(qkg4, kext4, vext4, kmean, bias):
    nb = qkg4.shape[0]
    grp = GROUP // LANES
    pairs = MOBA_HEADS // 2
    hps = HEADS_PER_STEP
    cur, nxt = _head_maps(nb, pairs)
    tiles = kmean.shape[0] // nb

    def kmean_spec(which, width):
        def index_map(bi, hp):
            b2, h2 = which(bi, hp)
            return b2, 0, h2
        return pl.BlockSpec((tiles, kmean.shape[1], width), index_map)

    return pl.pallas_call(
        _moba_kernel,
        grid=(nb, pairs // hps),
        in_specs=[_seq_spec(hps * LANES, cur, 3 * grp // hps), _seq_spec(LANES, nxt, 3 * grp),
                  _seq_spec(hps * 2 * LANES, cur, 0), _seq_spec(2 * LANES, nxt, 0),
                  _seq_spec(hps * 2 * LANES, cur, grp // hps), _seq_spec(hps * LANES, cur, 4 * grp // hps),
                  _bias_spec(cur, DIFF_HEADS // hps, hps), _bias_spec(nxt, DIFF_HEADS, 1),
                  kmean_spec(cur, hps * LANES), kmean_spec(nxt, LANES)],
        out_specs=pl.BlockSpec((NBLK * BLK, hps * LANES), lambda bi, hp: (bi, hp)),
        out_shape=jax.ShapeDtypeStruct((nb * NBLK * BLK, pairs * LANES), BF16),
        scratch_shapes=[pltpu.VMEM((LANES, LANES), BF16)] + _core_scratch(2 * LANES),
        compiler_params=pltpu.CompilerParams(
            dimension_semantics=("arbitrary", "arbitrary"), vmem_limit_bytes=VMEM_LIMIT),
        name="moba_attention",
    )(qkg4, qkg4, kext4, kext4, vext4, qkg4, bias, bias, kmean, kmean)


X_SLOTS = 3


def _outproj_kernel(a_ref, b_ref, x_hbm, g_ref, w_ref, o_ref, x_buf, x_sem):
    i = pl.program_id(0)
    nsteps = pl.num_programs(0)
    tm, half = a_ref.shape

    def x_copy(step):
        slot = step % X_SLOTS
        return pltpu.make_async_copy(x_hbm.at[pl.ds(step * tm, tm), :], x_buf.at[slot], x_sem.at[slot])

    @pl.when(i == 0)
    def _start_first():
        x_copy(0).start()

        @pl.when(nsteps > 1)
        def _():
            x_copy(1).start()

    @pl.when(i + 2 < nsteps)
    def _start_ahead():
        x_copy(i + 2).start()

    y = jnp.dot(a_ref[...], w_ref[:half, :], preferred_element_type=F32)
    y = y + jnp.dot(b_ref[...], w_ref[half:, :], preferred_element_type=F32)
    ms = jnp.mean(y * y, axis=-1, keepdims=True)
    y = y * lax.rsqrt(ms + NORM_EPS) * g_ref[...]
    x_copy(i).wait()
    o_ref[...] = x_buf[i % X_SLOTS] + y


def _outproj(a, b, x2d, g, w_bf16, tm=1024):
    m = x2d.shape[0]
    half = a.shape[1]
    return pl.pallas_call(
        _outproj_kernel,
        grid=(m // tm,),
        in_specs=[
            pl.BlockSpec((tm, half), lambda i: (i, 0)),
            pl.BlockSpec((tm, half), lambda i: (i, 0)),
            pl.BlockSpec(memory_space=pl.ANY),
            pl.BlockSpec((1, D_MODEL), lambda i: (0, 0)),
            pl.BlockSpec((2 * half, D_MODEL), lambda i: (0, 0)),
        ],
        out_specs=pl.BlockSpec((tm, D_MODEL), lambda i: (i, 0)),
        out_shape=jax.ShapeDtypeStruct((m, D_MODEL), F32),
        scratch_shapes=[pltpu.VMEM((X_SLOTS, tm, D_MODEL), F32), pltpu.SemaphoreType.DMA((X_SLOTS,))],
        compiler_params=pltpu.CompilerParams(
            dimension_semantics=("arbitrary",), vmem_limit_bytes=VMEM_LIMIT),
        name="outproj",
    )(a, b, x2d, g, w_bf16)


def kernel(x, norm_pre_g, w_in, diff_lambda, diff_subln_g, w_out, norm_post_g, rel_bias):
    b, s, d = x.shape
    assert d == D_MODEL and s == NBLK * BLK
    assert norm_pre_g.shape[0] == 1, "single-layer block"
    x2d = x.reshape(b * s, d)
    qkg, kext, vext, kmean, bias = _inproj(x2d, norm_pre_g[0:1], w_in[0], rel_bias)
    qkg4 = qkg.reshape(b, NBLK, BLK, QKG_WIDTH)
    kext4 = kext.reshape(b, NBLK, BLK, KEXT_WIDTH)
    vext4 = vext.reshape(b, NBLK, BLK, VEXT_WIDTH)
    a_out = _diff_attention(qkg4, vext4, bias, diff_lambda[0], diff_subln_g[0:1])
    b_out = _moba_attention(qkg4, kext4, vext4, kmean, bias)
    out = _outproj(a_out, b_out, x2d, norm_post_g[0:1], w_out[0].astype(BF16))
    return out.reshape(b, s, d)
```

```python
import math

import numpy as np
import jax
import jax.numpy as jnp
from jax import lax
from jax.experimental import pallas as pl
from jax.experimental.pallas import tpu as pltpu

F32 = jnp.float32
BF16 = jnp.bfloat16

D_MODEL = 1024
DIFF_HEADS = 4
DIFF_QK = 64
MOBA_HEADS = 8
MOBA_DIM = 64
MOBA_TOPK = 3
N_BUCKETS = 32
MAX_DISTANCE = 128
IN_WIDTH = 4096
GROUP = 512
BLK = 256
NBLK = 8
NSLOT = NBLK + 1
PAIR_GROUPS = ((0, 1), (2, 3))
GROUP_PAIRS = 2
LANES = 128
NORM_EPS = 1e-6
SUBLN_EPS = 1e-5
QK_SCALE = 0.125
LOG2E = math.log2(math.e)
MASK_BIG = 1e30
VMEM_LIMIT = 56 * 1024 * 1024

QKG_WIDTH = 5 * GROUP
KV_WIDTH = 3 * GROUP

_NT = (((1,), (1,)), ((), ()))


def _rel_bucket(n):
    max_exact = N_BUCKETS // 2
    nf = np.maximum(n, 1).astype(np.float32)
    ratio = np.log(nf / np.float32(max_exact)) / np.float32(math.log(MAX_DISTANCE / max_exact))
    large = max_exact + (ratio * np.float32(N_BUCKETS - max_exact)).astype(np.int32)
    large = np.minimum(large, N_BUCKETS - 1)
    return np.where(n < max_exact, n, large).astype(np.int32)


def _bucket_vectors():
    l = np.arange(2 * BLK)
    out = []
    for t in range(2):
        d = t * BLK + BLK - 1 - l
        out.append(np.where(d >= 0, _rel_bucket(np.maximum(d, 0)), -1).astype(np.int32))
    return np.stack(out)[:, None, :]


N_BIAS_GROUPS = DIFF_HEADS + MOBA_HEADS // 2


def _bias_group(tab_ref, bkt_ref, out_ref, g):
    for half in range(2):
        head = jnp.where(g < DIFF_HEADS, g, 2 * g - DIFF_HEADS + half)
        for t in range(2):
            bk = bkt_ref[t]
            vec = jnp.full(bk.shape, -jnp.inf, F32)
            for b in range(N_BUCKETS):
                vec = jnp.where(bk == b, tab_ref[b, head] * LOG2E, vec)
            rows = jnp.broadcast_to(vec, (BLK, 2 * BLK))
            rolled = pltpu.roll(rows, 1, 1, stride=1, stride_axis=0)
            out_ref[0, t, half * BLK:(half + 1) * BLK, :] = rolled[:, BLK:]
        out_ref[0, 2, half * BLK:(half + 1) * BLK, :] = jnp.full((BLK, BLK), tab_ref[N_BUCKETS - 1, head] * LOG2E, F32)


def _inproj_kernel(tab_ref, bkt_ref, x_ref, g_ref, w_ref, qkg_ref, kv_ref, kmean_ref, bias_ref, wbf_scr):
    tm = x_ref.shape[0]

    @pl.when(pl.program_id(0) == 0)
    def _cast_weights():
        for j in range(IN_WIDTH // GROUP):
            wbf_scr[:, j * GROUP:(j + 1) * GROUP] = w_ref[:, j * GROUP:(j + 1) * GROUP].astype(BF16)

    x = x_ref[...]
    ms = jnp.mean(x * x, axis=-1, keepdims=True)
    h = (x * lax.rsqrt(ms + NORM_EPS) * g_ref[...]).astype(BF16)

    def project(j):
        return jnp.dot(h, wbf_scr[:, j * GROUP:(j + 1) * GROUP], preferred_element_type=F32)

    mk = project(5)
    kv_ref[:, 0 * GROUP:1 * GROUP] = mk.astype(BF16)
    for r in range(tm // BLK):
        kmean_ref[0, r:r + 1, :] = jnp.sum(mk[r * BLK:(r + 1) * BLK], axis=0, keepdims=True) * (1.0 / BLK)
    kv_ref[:, 1 * GROUP:2 * GROUP] = project(2).astype(BF16)
    kv_ref[:, 2 * GROUP:3 * GROUP] = project(6).astype(BF16)

    qkg_ref[:, 0 * GROUP:1 * GROUP] = (project(0) * (QK_SCALE * LOG2E)).astype(BF16)
    qkg_ref[:, 1 * GROUP:2 * GROUP] = project(1).astype(BF16)
    qkg_ref[:, 2 * GROUP:3 * GROUP] = project(3).astype(BF16)
    qkg_ref[:, 3 * GROUP:4 * GROUP] = (project(4) * (QK_SCALE * LOG2E)).astype(BF16)
    qkg_ref[:, 4 * GROUP:5 * GROUP] = project(7).astype(BF16)

    _bias_group(tab_ref, bkt_ref, bias_ref, pl.program_id(0) // (pl.num_programs(0) // N_BIAS_GROUPS))


def _inproj(x2d, g, w_f32, rel_bias, tm=4 * BLK):
    m = x2d.shape[0]
    row = lambda i: (i, 0)
    bkt = jnp.asarray(_bucket_vectors())
    return pl.pallas_call(
        _inproj_kernel,
        grid=(m // tm,),
        in_specs=[
            pl.BlockSpec(memory_space=pltpu.SMEM),
            pl.BlockSpec(bkt.shape, lambda i: (0, 0, 0)),
            pl.BlockSpec((tm, D_MODEL), row),
            pl.BlockSpec((1, D_MODEL), lambda i: (0, 0)),
            pl.BlockSpec((D_MODEL, IN_WIDTH), lambda i: (0, 0), pipeline_mode=pl.Buffered(1)),
        ],
        out_specs=[
            pl.BlockSpec((tm, QKG_WIDTH), row),
            pl.BlockSpec((tm, KV_WIDTH), row),
            pl.BlockSpec((1, tm // BLK, GROUP), lambda i: (i, 0, 0)),
            pl.BlockSpec((1, 3, 2 * BLK, BLK), lambda i: (i // (m // tm // N_BIAS_GROUPS), 0, 0, 0)),
        ],
        out_shape=[
            jax.ShapeDtypeStruct((m, QKG_WIDTH), BF16),
            jax.ShapeDtypeStruct((m, KV_WIDTH), BF16),
            jax.ShapeDtypeStruct((m // tm, tm // BLK, GROUP), F32),
            jax.ShapeDtypeStruct((N_BIAS_GROUPS, 3, 2 * BLK, BLK), F32),
        ],
        scratch_shapes=[pltpu.VMEM((D_MODEL, IN_WIDTH), BF16)],
        compiler_params=pltpu.CompilerParams(
            dimension_semantics=("arbitrary",), vmem_limit_bytes=VMEM_LIMIT),
        name="inproj",
    )(rel_bias, bkt, x2d, g, w_f32)


def _slot_plan(pa, k):
    if k <= pa:
        return 0, k, min(pa - k, 2)
    kb = k - pa - 1
    return 1, kb, min(NBLK - 1 - pa - kb, 2)


def _emit_streams(scr, finish, scores=None, weighted=None):
    qe_scr, s_scr, m_scr = scr
    for j in range(GROUP_PAIRS):
        m_run = [None, None]
        acc_run = [None, None]
        for k in range(NSLOT):
            if scores is not None:
                g, k_at, bias_ref = scores
                pa = PAIR_GROUPS[g][j]
                sel, kb, idx = _slot_plan(pa, k)
                s = lax.dot_general(qe_scr[g, j, sel], k_at(kb), _NT, preferred_element_type=F32)
                s = s + bias_ref[0, idx]
                s_scr[g, j, k] = s
                t = jnp.maximum(s[:, :LANES], s[:, LANES:])
                m_run[sel] = t if m_run[sel] is None else jnp.maximum(m_run[sel], t)
                if kb == (pa, NBLK - 1 - pa)[sel]:
                    m_row = jnp.max(m_run[sel], axis=-1, keepdims=True)
                    m_scr[g, j, sel] = jnp.broadcast_to(m_row, (2 * BLK, LANES))
            if weighted is not None:
                g, v_at = weighted
                pa = PAIR_GROUPS[g][j]
                sel, kb, _ = _slot_plan(pa, k)
                mb = m_scr[g, j, sel]
                s = s_scr[g, j, k]
                p = jnp.concatenate([jnp.exp2(s[:, :LANES] - mb), jnp.exp2(s[:, LANES:] - mb)], axis=1).astype(BF16)
                pv = jnp.dot(p, v_at(kb), preferred_element_type=F32)
                acc_run[sel] = pv if acc_run[sel] is None else acc_run[sel] + pv
                if kb == (pa, NBLK - 1 - pa)[sel]:
                    acc = acc_run[sel]
                    finish((pa, NBLK - 1 - pa)[sel], acc[:, :LANES] / acc[:, LANES:])


def _region(trips, fn):
    def body(i, carry):
        fn()
        return carry
    lax.fori_loop(0, trips, body, 0)


def _pipelined_step(first, heads, scr):
    one = jnp.minimum(pl.program_id(0) + 1, 1)
    h0 = heads[0]

    def prologue():
        h0.build_cur(0)
        h0.build_cur(1)

    def prologue_scores():
        _emit_streams(scr, h0.finish, scores=(0, h0.k_cur, h0.bias_cur))

    _region(jnp.where(first, 1, 0), prologue)
    _region(jnp.where(first, 1, 0), prologue_scores)
    for hd in heads:
        def region1(hd=hd):
            hd.build_nxt(0)
            _emit_streams(scr, hd.finish, scores=(1, hd.k_cur, hd.bias_cur), weighted=(0, hd.v_cur))

        def region2(hd=hd):
            hd.build_nxt(1)
            _emit_streams(scr, hd.finish, scores=(0, hd.k_nxt, hd.bias_nxt), weighted=(1, hd.v_cur))

        _region(one, region1)
        _region(one, region2)


def _with_ones(v):
    return jnp.concatenate([v, jnp.ones(v.shape, v.dtype)], axis=1)


def _silu(g):
    return g / (1.0 + jnp.exp(-g))


def _core_scratch(kc):
    ng = len(PAIR_GROUPS)
    return [
        pltpu.VMEM((ng, GROUP_PAIRS, 2, 2 * BLK, kc), BF16),
        pltpu.VMEM((ng, GROUP_PAIRS, NSLOT, 2 * BLK, BLK), F32),
        pltpu.VMEM((ng, GROUP_PAIRS, 2, 2 * BLK, LANES), F32),
    ]


HEADS_PER_STEP = 2


class _Head:
    def __init__(self, **kw):
        self.__dict__.update(kw)


def _lanes(u, width=LANES):
    return slice(u * width, (u + 1) * width)


def _head_maps(nb, ncol):
    def cur(bi, hp):
        return bi, hp

    def nxt(bi, hp):
        flat = jnp.minimum((bi * (ncol // HEADS_PER_STEP) + hp + 1) * HEADS_PER_STEP, nb * ncol - 1)
        return flat // ncol, flat % ncol

    return cur, nxt


def _seq_spec(width, which, col0):
    def index_map(bi, hp):
        b2, h2 = which(bi, hp)
        return b2, 0, 0, col0 + h2
    return pl.BlockSpec((1, NBLK, BLK, width), index_map)


def _bias_spec(which, bias0, count):
    def index_map(bi, hp):
        _, h2 = which(bi, hp)
        return bias0 + h2, 0, 0, 0
    return pl.BlockSpec((count, 3, 2 * BLK, BLK), index_map)


def _diff_kernel(q2, q_nx, k2, k_nx, v2, g2, bias2, bias_nx, lam_ref, sg_ref, o_ref, *scr):
    qe_scr = scr[0]
    lane = lax.broadcasted_iota(jnp.int32, (BLK, LANES), 1)

    def build(q_at, g):
        for j, pa in enumerate(PAIR_GROUPS[g]):
            for t, qt in enumerate((pa, NBLK - 1 - pa)):
                q = q_at(qt).astype(F32)
                qe_scr[g, j, t, :BLK, :] = jnp.where(lane < DIFF_QK, q, 0.0).astype(BF16)
                qe_scr[g, j, t, BLK:, :] = jnp.where(lane >= DIFF_QK, q, 0.0).astype(BF16)

    def make_finish(u):
        def finish(qt, o):
            lp = lam_ref[...]
            lambda_init = 0.8 - 0.6 * math.exp(-0.3 * 0)
            lam = (jnp.exp(jnp.sum(lp[0:1] * lp[1:2], axis=-1, keepdims=True))
                   - jnp.exp(jnp.sum(lp[2:3] * lp[3:4], axis=-1, keepdims=True)) + lambda_init)
            out = o[:BLK] - lam * o[BLK:]
            ms = jnp.mean(out * out, axis=-1, keepdims=True)
            out = out * lax.rsqrt(ms + SUBLN_EPS) * sg_ref[...] * (1.0 - lambda_init)
            out = out * _silu(g2[0, qt][:, _lanes(u)].astype(F32))
            o_ref[qt * BLK:(qt + 1) * BLK, _lanes(u)] = out.astype(BF16)
        return finish

    def head(u):
        last = u == HEADS_PER_STEP - 1
        q_cur = lambda qt: q2[0, qt][:, _lanes(u)]
        q_nxt = (lambda qt: q_nx[0, qt]) if last else (lambda qt: q2[0, qt][:, _lanes(u + 1)])
        return _Head(
            build_cur=lambda g: build(q_cur, g), build_nxt=lambda g: build(q_nxt, g),
            k_cur=lambda kb: k2[0, kb][:, _lanes(u)],
            k_nxt=(lambda kb: k_nx[0, kb]) if last else (lambda kb: k2[0, kb][:, _lanes(u + 1)]),
            v_cur=lambda kb: _with_ones(v2[0, kb][:, _lanes(u)]),
            bias_cur=bias2.at[u:u + 1], bias_nxt=bias_nx if last else bias2.at[u + 1:u + 2],
            finish=make_finish(u))

    first = (pl.program_id(0) == 0) & (pl.program_id(1) == 0)
    _pipelined_step(first, [head(u) for u in range(HEADS_PER_STEP)], scr)


def _diff_attention(qkg4, kv4, bias, diff_lambda, subln_g):
    nb = qkg4.shape[0]
    grp = GROUP // LANES
    hps = HEADS_PER_STEP
    cur, nxt = _head_maps(nb, DIFF_HEADS)
    const = lambda bi, hp: (0, 0)
    return pl.pallas_call(
        _diff_kernel,
        grid=(nb, DIFF_HEADS // hps),
        in_specs=[_seq_spec(hps * LANES, cur, 0), _seq_spec(LANES, nxt, 0),
                  _seq_spec(hps * LANES, cur, grp // hps), _seq_spec(LANES, nxt, grp),
                  _seq_spec(hps * LANES, cur, grp // hps), _seq_spec(hps * LANES, cur, 2 * grp // hps),
                  _bias_spec(cur, 0, hps), _bias_spec(nxt, 0, 1),
                  pl.BlockSpec((4, DIFF_QK), const), pl.BlockSpec((1, LANES), const)],
        out_specs=pl.BlockSpec((NBLK * BLK, hps * LANES), lambda bi, hp: (bi, hp)),
        out_shape=jax.ShapeDtypeStruct((nb * NBLK * BLK, DIFF_HEADS * LANES), BF16),
        scratch_shapes=_core_scratch(LANES),
        compiler_params=pltpu.CompilerParams(
            dimension_semantics=("arbitrary", "arbitrary"), vmem_limit_bytes=VMEM_LIMIT),
        name="diff_attention",
    )(qkg4, qkg4, qkg4, qkg4, kv4, qkg4, bias, bias, diff_lambda, subln_g)


def _moba_kernel(q2, q_nx, k2, k_nx, v2, g2, bias2, bias_nx, kmean2, kmean_nx, o_ref, wg_scr, *scr):
    qe_scr = scr[0]
    lane = lax.broadcasted_iota(jnp.int32, (BLK, LANES), 1)
    n_iota = lax.broadcasted_iota(jnp.int32, (NBLK, BLK), 0)

    def gate_operand(kmean_at):
        wrow = lax.broadcasted_iota(jnp.int32, (LANES, LANES), 0)
        wlane = lax.broadcasted_iota(jnp.int32, (LANES, LANES), 1)
        wg = jnp.zeros((LANES, LANES), F32)
        for n in range(NBLK):
            kmean = kmean_at(n)
            hi = kmean.astype(BF16).astype(F32)
            mid = (kmean - hi).astype(BF16).astype(F32)
            lo = (kmean - hi - mid).astype(BF16).astype(F32)
            for piece, val in enumerate((hi, mid, lo)):
                for hh in range(2):
                    r = piece * 2 * NBLK + hh * NBLK + n
                    head_lanes = (wlane >= hh * MOBA_DIM) & (wlane < (hh + 1) * MOBA_DIM)
                    wg = jnp.where((wrow == r) & head_lanes, val, wg)
        return wg.astype(BF16)

    def build(q_at, wg, g):
        for j, pa in enumerate(PAIR_GROUPS[g]):
            for t, qt in enumerate((pa, NBLK - 1 - pa)):
                q = q_at(qt)
                qf = q.astype(F32)
                if qt == 0:
                    mask = jnp.zeros((BLK, LANES), F32)
                else:
                    g_t = lax.dot_general(wg, q, _NT, preferred_element_type=F32)
                    g_t = g_t[0:2 * NBLK] + g_t[2 * NBLK:4 * NBLK] + g_t[4 * NBLK:6 * NBLK]
                    mask_rows = []
                    for hh in range(2):
                        g8 = g_t[hh * NBLK:(hh + 1) * NBLK]
                        cnt = jnp.zeros((NBLK, BLK), jnp.int32)
                        for i in range(qt):
                            gi = g8[i:i + 1]
                            beats = (gi > g8) | ((gi == g8) & (i < n_iota))
                            cnt = cnt + jnp.where(beats, 1, 0)
                        keep = ((n_iota < qt) & (cnt < MOBA_TOPK)) | (n_iota == qt)
                        mask_rows.append(jnp.where(keep, 0.0, -MASK_BIG))
                    mask_t = jnp.concatenate(mask_rows + [jnp.zeros((LANES - 2 * NBLK, BLK), F32)], axis=0)
                    mask = mask_t.T
                for hh in range(2):
                    head_lanes = (lane >= hh * MOBA_DIM) & (lane < (hh + 1) * MOBA_DIM)
                    mask_lanes = (lane >= hh * NBLK) & (lane < (hh + 1) * NBLK)
                    qe_scr[g, j, t, hh * BLK:(hh + 1) * BLK, :LANES] = jnp.where(head_lanes, qf, 0.0).astype(BF16)
                    qe_scr[g, j, t, hh * BLK:(hh + 1) * BLK, LANES:] = jnp.where(mask_lanes, mask, 0.0).astype(BF16)

    def with_indicator(k, kb):
        onehot = jnp.where((lane == kb) | (lane == NBLK + kb), 1.0, 0.0).astype(BF16)
        return jnp.concatenate([k, onehot], axis=1)

    per_tile = kmean2.shape[1]

    def kmean_getter(ref, lanes):
        return lambda n: ref[n // per_tile, n % per_tile:n % per_tile + 1, lanes]

    def make_finish(u):
        def finish(qt, o):
            out = jnp.where(lane < MOBA_DIM, o[:BLK], o[BLK:])
            out = out * _silu(g2[0, qt][:, _lanes(u)].astype(F32))
            o_ref[qt * BLK:(qt + 1) * BLK, _lanes(u)] = out.astype(BF16)
        return finish

    def head(u):
        last = u == HEADS_PER_STEP - 1
        q_cur = lambda qt: q2[0, qt][:, _lanes(u)]
        q_nxt = (lambda qt: q_nx[0, qt]) if last else (lambda qt: q2[0, qt][:, _lanes(u + 1)])
        km_cur = kmean_getter(kmean2, _lanes(u))
        km_nxt = kmean_getter(kmean_nx, slice(None)) if last else kmean_getter(kmean2, _lanes(u + 1))

        def build_nxt(g):
            if g == 0:
                wg = gate_operand(km_nxt)
                wg_scr[...] = wg
            else:
                wg = wg_scr[...]
            build(q_nxt, wg, g)

        return _Head(
            build_cur=lambda g: build(q_cur, gate_operand(km_cur), g), build_nxt=build_nxt,
            k_cur=lambda kb: with_indicator(k2[0, kb][:, _lanes(u)], kb),
            k_nxt=(lambda kb: with_indicator(k_nx[0, kb], kb)) if last else (
                lambda kb: with_indicator(k2[0, kb][:, _lanes(u + 1)], kb)),
            v_cur=lambda kb: _with_ones(v2[0, kb][:, _lanes(u)]),
            bias_cur=bias2.at[u:u + 1], bias_nxt=bias_nx if last else bias2.at[u + 1:u + 2],
            finish=make_finish(u))

    first = (pl.program_id(0) == 0) & (pl.program_id(1) == 0)
    _pipelined_step(first, [head(u) for u in range(HEADS_PER_STEP)], scr)


def _moba_attention(qkg4, kv4, kmean, bias):
    nb = qkg4.shape[0]
    grp = GROUP // LANES
    pairs = MOBA_HEADS // 2
    hps = HEADS_PER_STEP
    cur, nxt = _head_maps(nb, pairs)
    tiles = kmean.shape[0] // nb

    def kmean_spec(which, width):
        def index_map(bi, hp):
            b2, h2 = which(bi, hp)
            return b2, 0, h2
        return pl.BlockSpec((tiles, kmean.shape[1], width), index_map)

    return pl.pallas_call(
        _moba_kernel,
        grid=(nb, pairs // hps),
        in_specs=[_seq_spec(hps * LANES, cur, 3 * grp // hps), _seq_spec(LANES, nxt, 3 * grp),
                  _seq_spec(hps * LANES, cur, 0), _seq_spec(LANES, nxt, 0),
                  _seq_spec(hps * LANES, cur, 2 * grp // hps), _seq_spec(hps * LANES, cur, 4 * grp // hps),
                  _bias_spec(cur, DIFF_HEADS // hps, hps), _bias_spec(nxt, DIFF_HEADS, 1),
                  kmean_spec(cur, hps * LANES), kmean_spec(nxt, LANES)],
        out_specs=pl.BlockSpec((NBLK * BLK, hps * LANES), lambda bi, hp: (bi, hp)),
        out_shape=jax.ShapeDtypeStruct((nb * NBLK * BLK, pairs * LANES), BF16),
        scratch_shapes=[pltpu.VMEM((LANES, LANES), BF16)] + _core_scratch(2 * LANES),
        compiler_params=pltpu.CompilerParams(
            dimension_semantics=("arbitrary", "arbitrary"), vmem_limit_bytes=VMEM_LIMIT),
        name="moba_attention",
    )(qkg4, qkg4, kv4, kv4, kv4, qkg4, bias, bias, kmean, kmean)


X_SLOTS = 3


def _outproj_kernel(a_ref, b_ref, x_hbm, g_ref, w_ref, o_ref, x_buf, x_sem):
    i = pl.program_id(0)
    nsteps = pl.num_programs(0)
    tm, half = a_ref.shape

    def x_copy(step):
        slot = step % X_SLOTS
        return pltpu.make_async_copy(x_hbm.at[pl.ds(step * tm, tm), :], x_buf.at[slot], x_sem.at[slot])

    @pl.when(i == 0)
    def _start_first():
        x_copy(0).start()

        @pl.when(nsteps > 1)
        def _():
            x_copy(1).start()

    @pl.when(i + 2 < nsteps)
    def _start_ahead():
        x_copy(i + 2).start()

    y = jnp.dot(a_ref[...], w_ref[:half, :], preferred_element_type=F32)
    y = y + jnp.dot(b_ref[...], w_ref[half:, :], preferred_element_type=F32)
    ms = jnp.mean(y * y, axis=-1, keepdims=True)
    y = y * lax.rsqrt(ms + NORM_EPS) * g_ref[...]
    x_copy(i).wait()
    o_ref[...] = x_buf[i % X_SLOTS] + y


def _outproj(a, b, x2d, g, w_bf16, tm=1024):
    m = x2d.shape[0]
    half = a.shape[1]
    return pl.pallas_call(
        _outproj_kernel,
        grid=(m // tm,),
        in_specs=[
            pl.BlockSpec((tm, half), lambda i: (i, 0)),
            pl.BlockSpec((tm, half), lambda i: (i, 0)),
            pl.BlockSpec(memory_space=pl.ANY),
            pl.BlockSpec((1, D_MODEL), lambda i: (0, 0)),
            pl.BlockSpec((2 * half, D_MODEL), lambda i: (0, 0)),
        ],
        out_specs=pl.BlockSpec((tm, D_MODEL), lambda i: (i, 0)),
        out_shape=jax.ShapeDtypeStruct((m, D_MODEL), F32),
        scratch_shapes=[pltpu.VMEM((X_SLOTS, tm, D_MODEL), F32), pltpu.SemaphoreType.DMA((X_SLOTS,))],
        compiler_params=pltpu.CompilerParams(
            dimension_semantics=("arbitrary",), vmem_limit_bytes=VMEM_LIMIT),
        name="outproj",
    )(a, b, x2d, g, w_bf16)


def kernel(x, norm_pre_g, w_in, diff_lambda, diff_subln_g, w_out, norm_post_g, rel_bias):
    b, s, d = x.shape
    assert d == D_MODEL and s == NBLK * BLK
    assert norm_pre_g.shape[0] == 1, "single-layer block"
    x2d = x.reshape(b * s, d)
    qkg, kv, kmean, bias = _inproj(x2d, norm_pre_g[0:1], w_in[0], rel_bias)
    qkg4 = qkg.reshape(b, NBLK, BLK, QKG_WIDTH)
    kv4 = kv.reshape(b, NBLK, BLK, KV_WIDTH)
    a_out = _diff_attention(qkg4, kv4, bias, diff_lambda[0], diff_subln_g[0:1])
    b_out = _moba_attention(qkg4, kv4, kmean, bias)
    out = _outproj(a_out, b_out, x2d, norm_post_g[0:1], w_out[0].astype(BF16))
    return out.reshape(b, s, d)
```

```python
import math

import numpy as np
import jax
import jax.numpy as jnp
from jax import lax
from jax.experimental import pallas as pl
from jax.experimental.pallas import tpu as pltpu

F32 = jnp.float32
BF16 = jnp.bfloat16

D_MODEL = 1024
DIFF_HEADS = 4
DIFF_QK = 64
MOBA_HEADS = 8
MOBA_DIM = 64
MOBA_TOPK = 3
N_BUCKETS = 32
MAX_DISTANCE = 128
IN_WIDTH = 4096
GROUP = 512
BLK = 256
NBLK = 8
NSLOT = NBLK + 1
PAIR_GROUPS = ((0, 1), (2, 3))
GROUP_PAIRS = 2
LANES = 128
NORM_EPS = 1e-6
SUBLN_EPS = 1e-5
QK_SCALE = 0.125
LOG2E = math.log2(math.e)
MASK_BIG = 1e30
VMEM_LIMIT = 56 * 1024 * 1024

QKG_WIDTH = 5 * GROUP
KV_WIDTH = 3 * GROUP

_NT = (((1,), (1,)), ((), ()))


def _rel_bucket(n):
    max_exact = N_BUCKETS // 2
    nf = np.maximum(n, 1).astype(np.float32)
    ratio = np.log(nf / np.float32(max_exact)) / np.float32(math.log(MAX_DISTANCE / max_exact))
    large = max_exact + (ratio * np.float32(N_BUCKETS - max_exact)).astype(np.int32)
    large = np.minimum(large, N_BUCKETS - 1)
    return np.where(n < max_exact, n, large).astype(np.int32)


def _bucket_vectors():
    l = np.arange(2 * BLK)
    out = []
    for t in range(2):
        d = t * BLK + BLK - 1 - l
        out.append(np.where(d >= 0, _rel_bucket(np.maximum(d, 0)), -1).astype(np.int32))
    return np.stack(out)[:, None, :]


N_BIAS_GROUPS = DIFF_HEADS + MOBA_HEADS // 2


def _bias_group(tab_ref, bkt_ref, out_ref, g):
    for half in range(2):
        head = jnp.where(g < DIFF_HEADS, g, 2 * g - DIFF_HEADS + half)
        for t in range(2):
            bk = bkt_ref[t]
            vec = jnp.full(bk.shape, -jnp.inf, F32)
            for b in range(N_BUCKETS):
                vec = jnp.where(bk == b, tab_ref[b, head] * LOG2E, vec)
            rows = jnp.broadcast_to(vec, (BLK, 2 * BLK))
            rolled = pltpu.roll(rows, 1, 1, stride=1, stride_axis=0)
            out_ref[0, t, half * BLK:(half + 1) * BLK, :] = rolled[:, BLK:]
        out_ref[0, 2, half * BLK:(half + 1) * BLK, :] = jnp.full((BLK, BLK), tab_ref[N_BUCKETS - 1, head] * LOG2E, F32)


def _inproj_kernel(tab_ref, bkt_ref, x_ref, g_ref, w_ref, qkg_ref, kv_ref, kmean_ref, bias_ref, wbf_scr):
    tm = x_ref.shape[0]

    @pl.when(pl.program_id(0) == 0)
    def _cast_weights():
        for j in range(IN_WIDTH // GROUP):
            wbf_scr[:, j * GROUP:(j + 1) * GROUP] = w_ref[:, j * GROUP:(j + 1) * GROUP].astype(BF16)

    x = x_ref[...]
    ms = jnp.mean(x * x, axis=-1, keepdims=True)
    h = (x * lax.rsqrt(ms + NORM_EPS) * g_ref[...]).astype(BF16)

    def project(j):
        return jnp.dot(h, wbf_scr[:, j * GROUP:(j + 1) * GROUP], preferred_element_type=F32)

    mk = project(5)
    kv_ref[:, 0 * GROUP:1 * GROUP] = mk.astype(BF16)
    for r in range(tm // BLK):
        kmean_ref[0, r:r + 1, :] = jnp.sum(mk[r * BLK:(r + 1) * BLK], axis=0, keepdims=True) * (1.0 / BLK)
    kv_ref[:, 1 * GROUP:2 * GROUP] = project(2).astype(BF16)
    kv_ref[:, 2 * GROUP:3 * GROUP] = project(6).astype(BF16)

    qkg_ref[:, 0 * GROUP:1 * GROUP] = (project(0) * (QK_SCALE * LOG2E)).astype(BF16)
    qkg_ref[:, 1 * GROUP:2 * GROUP] = project(1).astype(BF16)
    qkg_ref[:, 2 * GROUP:3 * GROUP] = project(3).astype(BF16)
    qkg_ref[:, 3 * GROUP:4 * GROUP] = (project(4) * (QK_SCALE * LOG2E)).astype(BF16)
    qkg_ref[:, 4 * GROUP:5 * GROUP] = project(7).astype(BF16)

    _bias_group(tab_ref, bkt_ref, bias_ref, pl.program_id(0) // (pl.num_programs(0) // N_BIAS_GROUPS))


def _inproj(x2d, g, w_f32, rel_bias, tm=4 * BLK):
    m = x2d.shape[0]
    row = lambda i: (i, 0)
    bkt = jnp.asarray(_bucket_vectors())
    return pl.pallas_call(
        _inproj_kernel,
        grid=(m // tm,),
        in_specs=[
            pl.BlockSpec(memory_space=pltpu.SMEM),
            pl.BlockSpec(bkt.shape, lambda i: (0, 0, 0)),
            pl.BlockSpec((tm, D_MODEL), row),
            pl.BlockSpec((1, D_MODEL), lambda i: (0, 0)),
            pl.BlockSpec((D_MODEL, IN_WIDTH), lambda i: (0, 0), pipeline_mode=pl.Buffered(1)),
        ],
        out_specs=[
            pl.BlockSpec((tm, QKG_WIDTH), row),
            pl.BlockSpec((tm, KV_WIDTH), row),
            pl.BlockSpec((1, tm // BLK, GROUP), lambda i: (i, 0, 0)),
            pl.BlockSpec((1, 3, 2 * BLK, BLK), lambda i: (i // (m // tm // N_BIAS_GROUPS), 0, 0, 0)),
        ],
        out_shape=[
            jax.ShapeDtypeStruct((m, QKG_WIDTH), BF16),
            jax.ShapeDtypeStruct((m, KV_WIDTH), BF16),
            jax.ShapeDtypeStruct((m // tm, tm // BLK, GROUP), F32),
            jax.ShapeDtypeStruct((N_BIAS_GROUPS, 3, 2 * BLK, BLK), F32),
        ],
        scratch_shapes=[pltpu.VMEM((D_MODEL, IN_WIDTH), BF16)],
        compiler_params=pltpu.CompilerParams(
            dimension_semantics=("arbitrary",), vmem_limit_bytes=VMEM_LIMIT),
        name="inproj",
    )(rel_bias, bkt, x2d, g, w_f32)


def _slot_plan(pa, k):
    if k <= pa:
        return 0, k, min(pa - k, 2)
    kb = k - pa - 1
    return 1, kb, min(NBLK - 1 - pa - kb, 2)


def _emit_streams(scr, finish, scores=None, weighted=None):
    qe_scr, s_scr, m_scr = scr
    for j in range(GROUP_PAIRS):
        m_run = [None, None]
        acc_run = [None, None]
        for k in range(NSLOT):
            if scores is not None:
                g, k_at, bias_ref = scores
                pa = PAIR_GROUPS[g][j]
                sel, kb, idx = _slot_plan(pa, k)
                s = lax.dot_general(qe_scr[g, j, sel], k_at(kb), _NT, preferred_element_type=F32)
                s = s + bias_ref[0, idx]
                s_scr[g, j, k] = s
                t = jnp.maximum(s[:, :LANES], s[:, LANES:])
                m_run[sel] = t if m_run[sel] is None else jnp.maximum(m_run[sel], t)
                if kb == (pa, NBLK - 1 - pa)[sel]:
                    m_row = jnp.max(m_run[sel], axis=-1, keepdims=True)
                    m_scr[g, j, sel] = jnp.broadcast_to(m_row, (2 * BLK, LANES))
            if weighted is not None:
                g, v_at = weighted
                pa = PAIR_GROUPS[g][j]
                sel, kb, _ = _slot_plan(pa, k)
                mb = m_scr[g, j, sel]
                s = s_scr[g, j, k]
                p = jnp.concatenate([jnp.exp2(s[:, :LANES] - mb), jnp.exp2(s[:, LANES:] - mb)], axis=1).astype(BF16)
                pv = jnp.dot(p, v_at(kb), preferred_element_type=F32)
                acc_run[sel] = pv if acc_run[sel] is None else acc_run[sel] + pv
                if kb == (pa, NBLK - 1 - pa)[sel]:
                    acc = acc_run[sel]
                    finish((pa, NBLK - 1 - pa)[sel], acc[:, :LANES] / acc[:, LANES:])


def _region(trips, fn):
    def body(i, carry):
        fn()
        return carry
    lax.fori_loop(0, trips, body, 0)


def _pipelined_step(first, heads, scr):
    one = jnp.minimum(pl.program_id(0) + 1, 1)
    h0 = heads[0]

    def prologue():
        h0.build_cur(0)
        h0.build_cur(1)

    def prologue_scores():
        _emit_streams(scr, h0.finish, scores=(0, h0.k_cur, h0.bias_cur))

    _region(jnp.where(first, 1, 0), prologue)
    _region(jnp.where(first, 1, 0), prologue_scores)
    for hd in heads:
        def region1(hd=hd):
            hd.build_nxt(0)
            _emit_streams(scr, hd.finish, scores=(1, hd.k_cur, hd.bias_cur), weighted=(0, hd.v_cur))

        def region2(hd=hd):
            hd.build_nxt(1)
            _emit_streams(scr, hd.finish, scores=(0, hd.k_nxt, hd.bias_nxt), weighted=(1, hd.v_cur))

        _region(one, region1)
        _region(one, region2)


def _with_ones(v):
    return jnp.concatenate([v, jnp.ones(v.shape, v.dtype)], axis=1)


def _silu(g):
    return g / (1.0 + jnp.exp(-g))


def _core_scratch(kc):
    ng = len(PAIR_GROUPS)
    return [
        pltpu.VMEM((ng, GROUP_PAIRS, 2, 2 * BLK, kc), BF16),
        pltpu.VMEM((ng, GROUP_PAIRS, NSLOT, 2 * BLK, BLK), F32),
        pltpu.VMEM((ng, GROUP_PAIRS, 2, 2 * BLK, LANES), F32),
    ]


HEADS_PER_STEP = 2


class _Head:
    def __init__(self, **kw):
        self.__dict__.update(kw)


def _lanes(u, width=LANES):
    return slice(u * width, (u + 1) * width)


def _head_maps(nb, ncol):
    def cur(bi, hp):
        return bi, hp

    def nxt(bi, hp):
        flat = jnp.minimum((bi * (ncol // HEADS_PER_STEP) + hp + 1) * HEADS_PER_STEP, nb * ncol - 1)
        return flat // ncol, flat % ncol

    return cur, nxt


def _seq_spec(width, which, col0):
    def index_map(bi, hp):
        b2, h2 = which(bi, hp)
        return b2, 0, 0, col0 + h2
    return pl.BlockSpec((1, NBLK, BLK, width), index_map)


def _bias_spec(which, bias0, count):
    def index_map(bi, hp):
        _, h2 = which(bi, hp)
        return bias0 + h2, 0, 0, 0
    return pl.BlockSpec((count, 3, 2 * BLK, BLK), index_map)


def _diff_kernel(q2, q_nx, k2, k_nx, v2, g2, bias2, bias_nx, lam_ref, sg_ref, o_ref, *scr):
    qe_scr = scr[0]
    lane = lax.broadcasted_iota(jnp.int32, (BLK, LANES), 1)

    def build(q_at, g):
        for j, pa in enumerate(PAIR_GROUPS[g]):
            for t, qt in enumerate((pa, NBLK - 1 - pa)):
                q = q_at(qt).astype(F32)
                qe_scr[g, j, t, :BLK, :] = jnp.where(lane < DIFF_QK, q, 0.0).astype(BF16)
                qe_scr[g, j, t, BLK:, :] = jnp.where(lane >= DIFF_QK, q, 0.0).astype(BF16)

    def make_finish(u):
        def finish(qt, o):
            lp = lam_ref[...]
            lambda_init = 0.8 - 0.6 * math.exp(-0.3 * 0)
            lam = (jnp.exp(jnp.sum(lp[0:1] * lp[1:2], axis=-1, keepdims=True))
                   - jnp.exp(jnp.sum(lp[2:3] * lp[3:4], axis=-1, keepdims=True)) + lambda_init)
            out = o[:BLK] - lam * o[BLK:]
            ms = jnp.mean(out * out, axis=-1, keepdims=True)
            out = out * lax.rsqrt(ms + SUBLN_EPS) * sg_ref[...] * (1.0 - lambda_init)
            out = out * _silu(g2[0, qt][:, _lanes(u)].astype(F32))
            o_ref[qt * BLK:(qt + 1) * BLK, _lanes(u)] = out.astype(BF16)
        return finish

    def head(u):
        last = u == HEADS_PER_STEP - 1
        q_cur = lambda qt: q2[0, qt][:, _lanes(u)]
        q_nxt = (lambda qt: q_nx[0, qt]) if last else (lambda qt: q2[0, qt][:, _lanes(u + 1)])
        return _Head(
            build_cur=lambda g: build(q_cur, g), build_nxt=lambda g: build(q_nxt, g),
            k_cur=lambda kb: k2[0, kb][:, _lanes(u)],
            k_nxt=(lambda kb: k_nx[0, kb]) if last else (lambda kb: k2[0, kb][:, _lanes(u + 1)]),
            v_cur=lambda kb: _with_ones(v2[0, kb][:, _lanes(u)]),
            bias_cur=bias2.at[u:u + 1], bias_nxt=bias_nx if last else bias2.at[u + 1:u + 2],
            finish=make_finish(u))

    first = (pl.program_id(0) == 0) & (pl.program_id(1) == 0)
    _pipelined_step(first, [head(u) for u in range(HEADS_PER_STEP)], scr)


def _diff_attention(qkg4, kv4, bias, diff_lambda, subln_g):
    nb = qkg4.shape[0]
    grp = GROUP // LANES
    hps = HEADS_PER_STEP
    cur, nxt = _head_maps(nb, DIFF_HEADS)
    const = lambda bi, hp: (0, 0)
    return pl.pallas_call(
        _diff_kernel,
        grid=(nb, DIFF_HEADS // hps),
        in_specs=[_seq_spec(hps * LANES, cur, 0), _seq_spec(LANES, nxt, 0),
                  _seq_spec(hps * LANES, cur, grp // hps), _seq_spec(LANES, nxt, grp),
                  _seq_spec(hps * LANES, cur, grp // hps), _seq_spec(hps * LANES, cur, 2 * grp // hps),
                  _bias_spec(cur, 0, hps), _bias_spec(nxt, 0, 1),
                  pl.BlockSpec((4, DIFF_QK), const), pl.BlockSpec((1, LANES), const)],
        out_specs=pl.BlockSpec((NBLK * BLK, hps * LANES), lambda bi, hp: (bi, hp)),
        out_shape=jax.ShapeDtypeStruct((nb * NBLK * BLK, DIFF_HEADS * LANES), BF16),
        scratch_shapes=_core_scratch(LANES),
        compiler_params=pltpu.CompilerParams(
            dimension_semantics=("arbitrary", "arbitrary"), vmem_limit_bytes=VMEM_LIMIT),
        name="diff_attention",
    )(qkg4, qkg4, qkg4, qkg4, kv4, qkg4, bias, bias, diff_lambda, subln_g)


def _moba_kernel(q2, q_nx, k2, k_nx, v2, g2, bias2, bias_nx, kmean2, kmean_nx, o_ref, wg_scr, *scr):
    qe_scr = scr[0]
    lane = lax.broadcasted_iota(jnp.int32, (BLK, LANES), 1)
    n_iota = lax.broadcasted_iota(jnp.int32, (NBLK, BLK), 0)

    def gate_operand(kmean_at):
        wrow = lax.broadcasted_iota(jnp.int32, (LANES, LANES), 0)
        wlane = lax.broadcasted_iota(jnp.int32, (LANES, LANES), 1)
        wg = jnp.zeros((LANES, LANES), F32)
        for n in range(NBLK):
            kmean = kmean_at(n)
            hi = kmean.astype(BF16).astype(F32)
            mid = (kmean - hi).astype(BF16).astype(F32)
            lo = (kmean - hi - mid).astype(BF16).astype(F32)
            for piece, val in enumerate((hi, mid, lo)):
                for hh in range(2):
                    r = piece * 2 * NBLK + hh * NBLK + n
                    head_lanes = (wlane >= hh * MOBA_DIM) & (wlane < (hh + 1) * MOBA_DIM)
                    wg = jnp.where((wrow == r) & head_lanes, val, wg)
        return wg.astype(BF16)

    def build(q_at, wg, g):
        for j, pa in enumerate(PAIR_GROUPS[g]):
            for t, qt in enumerate((pa, NBLK - 1 - pa)):
                q = q_at(qt)
                qf = q.astype(F32)
                if qt == 0:
                    mask = jnp.zeros((BLK, LANES), F32)
                else:
                    g_t = lax.dot_general(wg, q, _NT, preferred_element_type=F32)
                    g_t = g_t[0:2 * NBLK] + g_t[2 * NBLK:4 * NBLK] + g_t[4 * NBLK:6 * NBLK]
                    mask_rows = []
                    for hh in range(2):
                        g8 = g_t[hh * NBLK:(hh + 1) * NBLK]
                        cnt = jnp.zeros((NBLK, BLK), jnp.int32)
                        for i in range(qt):
                            gi = g8[i:i + 1]
                            beats = (gi > g8) | ((gi == g8) & (i < n_iota))
                            cnt = cnt + jnp.where(beats, 1, 0)
                        keep = ((n_iota < qt) & (cnt < MOBA_TOPK)) | (n_iota == qt)
                        mask_rows.append(jnp.where(keep, 0.0, -MASK_BIG))
                    mask_t = jnp.concatenate(mask_rows + [jnp.zeros((LANES - 2 * NBLK, BLK), F32)], axis=0)
                    mask = mask_t.T
                for hh in range(2):
                    head_lanes = (lane >= hh * MOBA_DIM) & (lane < (hh + 1) * MOBA_DIM)
                    mask_lanes = (lane >= hh * NBLK) & (lane < (hh + 1) * NBLK)
                    qe_scr[g, j, t, hh * BLK:(hh + 1) * BLK, :LANES] = jnp.where(head_lanes, qf, 0.0).astype(BF16)
                    qe_scr[g, j, t, hh * BLK:(hh + 1) * BLK, LANES:] = jnp.where(mask_lanes, mask, 0.0).astype(BF16)

    def with_indicator(k, kb):
        onehot = jnp.where((lane == kb) | (lane == NBLK + kb), 1.0, 0.0).astype(BF16)
        return jnp.concatenate([k, onehot], axis=1)

    per_tile = kmean2.shape[1]

    def kmean_getter(ref, lanes):
        return lambda n: ref[n // per_tile, n % per_tile:n % per_tile + 1, lanes]

    def make_finish(u):
        def finish(qt, o):
            out = jnp.where(lane < MOBA_DIM, o[:BLK], o[BLK:])
            out = out * _silu(g2[0, qt][:, _lanes(u)].astype(F32))
            o_ref[qt * BLK:(qt + 1) * BLK, _lanes(u)] = out.astype(BF16)
        return finish

    def head(u):
        last = u == HEADS_PER_STEP - 1
        q_cur = lambda qt: q2[0, qt][:, _lanes(u)]
        q_nxt = (lambda qt: q_nx[0, qt]) if last else (lambda qt: q2[0, qt][:, _lanes(u + 1)])
        km_cur = kmean_getter(kmean2, _lanes(u))
        km_nxt = kmean_getter(kmean_nx, slice(None)) if last else kmean_getter(kmean2, _lanes(u + 1))

        def build_nxt(g):
            if g == 0:
                wg = gate_operand(km_nxt)
                wg_scr[...] = wg
            else:
                wg = wg_scr[...]
            build(q_nxt, wg, g)

        return _Head(
            build_cur=lambda g: build(q_cur, gate_operand(km_cur), g), build_nxt=build_nxt,
            k_cur=lambda kb: with_indicator(k2[0, kb][:, _lanes(u)], kb),
            k_nxt=(lambda kb: with_indicator(k_nx[0, kb], kb)) if last else (
                lambda kb: with_indicator(k2[0, kb][:, _lanes(u + 1)], kb)),
            v_cur=lambda kb: _with_ones(v2[0, kb][:, _lanes(u)]),
            bias_cur=bias2.at[u:u + 1], bias_nxt=bias_nx if last else bias2.at[u + 1:u + 2],
            finish=make_finish(u))

    first = (pl.program_id(0) == 0) & (pl.program_id(1) == 0)
    _pipelined_step(first, [head(u) for u in range(HEADS_PER_STEP)], scr)


def _moba_attention(qkg4, kv4, kmean, bias):
    nb = qkg4.shape[0]
    grp = GROUP // LANES
    pairs = MOBA_HEADS // 2
    hps = HEADS_PER_STEP
    cur, nxt = _head_maps(nb, pairs)
    tiles = kmean.shape[0] // nb

    def kmean_spec(which, width):
        def index_map(bi, hp):
            b2, h2 = which(bi, hp)
            return b2, 0, h2
        return pl.BlockSpec((tiles, kmean.shape[1], width), index_map)

    return pl.pallas_call(
        _moba_kernel,
        grid=(nb, pairs // hps),
        in_specs=[_seq_spec(hps * LANES, cur, 3 * grp // hps), _seq_spec(LANES, nxt, 3 * grp),
                  _seq_spec(hps * LANES, cur, 0), _seq_spec(LANES, nxt, 0),
                  _seq_spec(hps * LANES, cur, 2 * grp // hps), _seq_spec(hps * LANES, cur, 4 * grp // hps),
                  _bias_spec(cur, DIFF_HEADS // hps, hps), _bias_spec(nxt, DIFF_HEADS, 1),
                  kmean_spec(cur, hps * LANES), kmean_spec(nxt, LANES)],
        out_specs=pl.BlockSpec((NBLK * BLK, hps * LANES), lambda bi, hp: (bi, hp)),
        out_shape=jax.ShapeDtypeStruct((nb * NBLK * BLK, pairs * LANES), BF16),
        scratch_shapes=[pltpu.VMEM((LANES, LANES), BF16)] + _core_scratch(2 * LANES),
        compiler_params=pltpu.CompilerParams(
            dimension_semantics=("arbitrary", "arbitrary"), vmem_limit_bytes=VMEM_LIMIT),
        name="moba_attention",
    )(qkg4, qkg4, kv4, kv4, kv4, qkg4, bias, bias, kmean, kmean)


X_SLOTS = 3
ROW_CHUNKS = 2


def _outproj_kernel(a_ref, b_ref, x_hbm, g_ref, w_ref, o_ref, x_buf, x_sem):
    i = pl.program_id(0)
    nsteps = pl.num_programs(0)
    tm, half = a_ref.shape

    def x_copy(step):
        slot = step % X_SLOTS
        return pltpu.make_async_copy(x_hbm.at[pl.ds(step * tm, tm), :], x_buf.at[slot], x_sem.at[slot])

    @pl.when(i == 0)
    def _start_first():
        x_copy(0).start()

        @pl.when(nsteps > 1)
        def _():
            x_copy(1).start()

    @pl.when(i + 2 < nsteps)
    def _start_ahead():
        x_copy(i + 2).start()

    x_copy(i).wait()
    sub = tm // ROW_CHUNKS
    for c in range(ROW_CHUNKS):
        rows = slice(c * sub, (c + 1) * sub)
        y = jnp.dot(a_ref[rows, :], w_ref[:half, :], preferred_element_type=F32)
        y = y + jnp.dot(b_ref[rows, :], w_ref[half:, :], preferred_element_type=F32)
        ms = jnp.mean(y * y, axis=-1, keepdims=True)
        o_ref[rows, :] = x_buf[i % X_SLOTS, rows, :] + y * lax.rsqrt(ms + NORM_EPS) * g_ref[...]


def _outproj(a, b, x2d, g, w_bf16, tm=1024):
    m = x2d.shape[0]
    half = a.shape[1]
    return pl.pallas_call(
        _outproj_kernel,
        grid=(m // tm,),
        in_specs=[
            pl.BlockSpec((tm, half), lambda i: (i, 0)),
            pl.BlockSpec((tm, half), lambda i: (i, 0)),
            pl.BlockSpec(memory_space=pl.ANY),
            pl.BlockSpec((1, D_MODEL), lambda i: (0, 0)),
            pl.BlockSpec((2 * half, D_MODEL), lambda i: (0, 0)),
        ],
        out_specs=pl.BlockSpec((tm, D_MODEL), lambda i: (i, 0)),
        out_shape=jax.ShapeDtypeStruct((m, D_MODEL), F32),
        scratch_shapes=[pltpu.VMEM((X_SLOTS, tm, D_MODEL), F32), pltpu.SemaphoreType.DMA((X_SLOTS,))],
        compiler_params=pltpu.CompilerParams(
            dimension_semantics=("arbitrary",), vmem_limit_bytes=VMEM_LIMIT),
        name="outproj",
    )(a, b, x2d, g, w_bf16)


def kernel(x, norm_pre_g, w_in, diff_lambda, diff_subln_g, w_out, norm_post_g, rel_bias):
    b, s, d = x.shape
    assert d == D_MODEL and s == NBLK * BLK
    assert norm_pre_g.shape[0] == 1, "single-layer block"
    x2d = x.reshape(b * s, d)
    qkg, kv, kmean, bias = _inproj(x2d, norm_pre_g[0:1], w_in[0], rel_bias)
    qkg4 = qkg.reshape(b, NBLK, BLK, QKG_WIDTH)
    kv4 = kv.reshape(b, NBLK, BLK, KV_WIDTH)
    a_out = _diff_attention(qkg4, kv4, bias, diff_lambda[0], diff_subln_g[0:1])
    b_out = _moba_attention(qkg4, kv4, kmean, bias)
    out = _outproj(a_out, b_out, x2d, norm_post_g[0:1], w_out[0].astype(BF16))
    return out.reshape(b, s, d)
```

```python
import math

import numpy as np
import jax
import jax.numpy as jnp
from jax import lax
from jax.experimental import pallas as pl
from jax.experimental.pallas import tpu as pltpu

F32 = jnp.float32
BF16 = jnp.bfloat16

D_MODEL = 1024
DIFF_HEADS = 4
DIFF_QK = 64
MOBA_HEADS = 8
MOBA_DIM = 64
MOBA_TOPK = 3
N_BUCKETS = 32
MAX_DISTANCE = 128
IN_WIDTH = 4096
GROUP = 512
BLK = 256
NBLK = 8
NSLOT = NBLK + 1
PAIR_GROUPS = ((0, 1), (2, 3))
GROUP_PAIRS = 2
LANES = 128
NORM_EPS = 1e-6
SUBLN_EPS = 1e-5
QK_SCALE = 0.125
LOG2E = math.log2(math.e)
MASK_BIG = 1e30
VMEM_LIMIT = 56 * 1024 * 1024

QKG_WIDTH = 5 * GROUP
KV_WIDTH = 3 * GROUP

_NT = (((1,), (1,)), ((), ()))


def _rel_bucket(n):
    max_exact = N_BUCKETS // 2
    nf = np.maximum(n, 1).astype(np.float32)
    ratio = np.log(nf / np.float32(max_exact)) / np.float32(math.log(MAX_DISTANCE / max_exact))
    large = max_exact + (ratio * np.float32(N_BUCKETS - max_exact)).astype(np.int32)
    large = np.minimum(large, N_BUCKETS - 1)
    return np.where(n < max_exact, n, large).astype(np.int32)


def _bucket_vectors():
    l = np.arange(2 * BLK)
    out = []
    for t in range(2):
        d = t * BLK + BLK - 1 - l
        out.append(np.where(d >= 0, _rel_bucket(np.maximum(d, 0)), -1).astype(np.int32))
    return np.stack(out)[:, None, :]


N_BIAS_GROUPS = DIFF_HEADS + MOBA_HEADS // 2


def _bias_group(tab_ref, bkt_ref, out_ref, g):
    for half in range(2):
        head = jnp.where(g < DIFF_HEADS, g, 2 * g - DIFF_HEADS + half)
        for t in range(2):
            bk = bkt_ref[t]
            vec = jnp.full(bk.shape, -jnp.inf, F32)
            for b in range(N_BUCKETS):
                vec = jnp.where(bk == b, tab_ref[b, head] * LOG2E, vec)
            rows = jnp.broadcast_to(vec, (BLK, 2 * BLK))
            rolled = pltpu.roll(rows, 1, 1, stride=1, stride_axis=0)
            out_ref[0, t, half * BLK:(half + 1) * BLK, :] = rolled[:, BLK:]
        out_ref[0, 2, half * BLK:(half + 1) * BLK, :] = jnp.full((BLK, BLK), tab_ref[N_BUCKETS - 1, head] * LOG2E, F32)


def _inproj_kernel(tab_ref, bkt_ref, x_ref, g_ref, w_ref, qkg_ref, kv_ref, kmean_ref, bias_ref, wbf_scr):
    tm = x_ref.shape[0]

    @pl.when(pl.program_id(0) == 0)
    def _cast_weights():
        for j in range(IN_WIDTH // GROUP):
            wbf_scr[:, j * GROUP:(j + 1) * GROUP] = w_ref[:, j * GROUP:(j + 1) * GROUP].astype(BF16)

    x = x_ref[...]
    ms = jnp.mean(x * x, axis=-1, keepdims=True)
    h = (x * lax.rsqrt(ms + NORM_EPS) * g_ref[...]).astype(BF16)

    def project(j):
        return jnp.dot(h, wbf_scr[:, j * GROUP:(j + 1) * GROUP], preferred_element_type=F32)

    mk = project(5)
    kv_ref[:, 0 * GROUP:1 * GROUP] = mk.astype(BF16)
    for r in range(tm // BLK):
        kmean_ref[0, r:r + 1, :] = jnp.sum(mk[r * BLK:(r + 1) * BLK], axis=0, keepdims=True) * (1.0 / BLK)
    kv_ref[:, 1 * GROUP:2 * GROUP] = project(2).astype(BF16)
    kv_ref[:, 2 * GROUP:3 * GROUP] = project(6).astype(BF16)

    qkg_ref[:, 0 * GROUP:1 * GROUP] = (project(0) * (QK_SCALE * LOG2E)).astype(BF16)
    qkg_ref[:, 1 * GROUP:2 * GROUP] = project(1).astype(BF16)
    qkg_ref[:, 2 * GROUP:3 * GROUP] = project(3).astype(BF16)
    qkg_ref[:, 3 * GROUP:4 * GROUP] = (project(4) * (QK_SCALE * LOG2E)).astype(BF16)
    qkg_ref[:, 4 * GROUP:5 * GROUP] = project(7).astype(BF16)

    _bias_group(tab_ref, bkt_ref, bias_ref, pl.program_id(0) // (pl.num_programs(0) // N_BIAS_GROUPS))


def _inproj(x2d, g, w_f32, rel_bias, tm=4 * BLK):
    m = x2d.shape[0]
    row = lambda i: (i, 0)
    bkt = jnp.asarray(_bucket_vectors())
    return pl.pallas_call(
        _inproj_kernel,
        grid=(m // tm,),
        in_specs=[
            pl.BlockSpec(memory_space=pltpu.SMEM),
            pl.BlockSpec(bkt.shape, lambda i: (0, 0, 0)),
            pl.BlockSpec((tm, D_MODEL), row),
            pl.BlockSpec((1, D_MODEL), lambda i: (0, 0)),
            pl.BlockSpec((D_MODEL, IN_WIDTH), lambda i: (0, 0), pipeline_mode=pl.Buffered(1)),
        ],
        out_specs=[
            pl.BlockSpec((tm, QKG_WIDTH), row),
            pl.BlockSpec((tm, KV_WIDTH), row),
            pl.BlockSpec((1, tm // BLK, GROUP), lambda i: (i, 0, 0)),
            pl.BlockSpec((1, 3, 2 * BLK, BLK), lambda i: (i // (m // tm // N_BIAS_GROUPS), 0, 0, 0)),
        ],
        out_shape=[
            jax.ShapeDtypeStruct((m, QKG_WIDTH), BF16),
            jax.ShapeDtypeStruct((m, KV_WIDTH), BF16),
            jax.ShapeDtypeStruct((m // tm, tm // BLK, GROUP), F32),
            jax.ShapeDtypeStruct((N_BIAS_GROUPS, 3, 2 * BLK, BLK), F32),
        ],
        scratch_shapes=[pltpu.VMEM((D_MODEL, IN_WIDTH), BF16)],
        compiler_params=pltpu.CompilerParams(
            dimension_semantics=("arbitrary",), vmem_limit_bytes=VMEM_LIMIT),
        name="inproj",
    )(rel_bias, bkt, x2d, g, w_f32)


def _slot_plan(pa, k):
    if k <= pa:
        return 0, k, min(pa - k, 2)
    kb = k - pa - 1
    return 1, kb, min(NBLK - 1 - pa - kb, 2)


def _emit_streams(scr, finish, scores=None, weighted=None):
    qe_scr, s_scr, m_scr = scr
    for j in range(GROUP_PAIRS):
        m_run = [None, None]
        acc_run = [None, None]
        for k in range(NSLOT):
            if scores is not None:
                g, k_at, bias_ref = scores
                pa = PAIR_GROUPS[g][j]
                sel, kb, idx = _slot_plan(pa, k)
                s = lax.dot_general(qe_scr[g, j, sel], k_at(kb), _NT, preferred_element_type=F32)
                s = s + bias_ref[0, idx]
                s_scr[g, j, k] = s
                t = jnp.maximum(s[:, :LANES], s[:, LANES:])
                m_run[sel] = t if m_run[sel] is None else jnp.maximum(m_run[sel], t)
                if kb == (pa, NBLK - 1 - pa)[sel]:
                    m_row = jnp.max(m_run[sel], axis=-1, keepdims=True)
                    m_scr[g, j, sel] = jnp.broadcast_to(m_row, (2 * BLK, LANES))
            if weighted is not None:
                g, v_at = weighted
                pa = PAIR_GROUPS[g][j]
                sel, kb, _ = _slot_plan(pa, k)
                mb = m_scr[g, j, sel]
                s = s_scr[g, j, k]
                p = jnp.concatenate([jnp.exp2(s[:, :LANES] - mb), jnp.exp2(s[:, LANES:] - mb)], axis=1).astype(BF16)
                pv = jnp.dot(p, v_at(kb), preferred_element_type=F32)
                acc_run[sel] = pv if acc_run[sel] is None else acc_run[sel] + pv
                if kb == (pa, NBLK - 1 - pa)[sel]:
                    acc = acc_run[sel]
                    finish((pa, NBLK - 1 - pa)[sel], acc[:, :LANES] / acc[:, LANES:])


def _region(trips, fn):
    def body(i, carry):
        fn()
        return carry
    lax.fori_loop(0, trips, body, 0)


def _pipelined_step(first, heads, scr):
    one = jnp.minimum(pl.program_id(0) + 1, 1)
    h0 = heads[0]

    def prologue():
        h0.build_cur(0)
        h0.build_cur(1)

    def prologue_scores():
        _emit_streams(scr, h0.finish, scores=(0, h0.k_cur, h0.bias_cur))

    _region(jnp.where(first, 1, 0), prologue)
    _region(jnp.where(first, 1, 0), prologue_scores)
    for hd in heads:
        def region1(hd=hd):
            hd.build_nxt(0)
            _emit_streams(scr, hd.finish, scores=(1, hd.k_cur, hd.bias_cur), weighted=(0, hd.v_cur))

        def region2(hd=hd):
            hd.build_nxt(1)
            _emit_streams(scr, hd.finish, scores=(0, hd.k_nxt, hd.bias_nxt), weighted=(1, hd.v_cur))

        _region(one, region1)
        _region(one, region2)


def _with_ones(v):
    return jnp.concatenate([v, jnp.ones(v.shape, v.dtype)], axis=1)


def _silu(g):
    return g / (1.0 + jnp.exp(-g))


def _core_scratch(kc):
    ng = len(PAIR_GROUPS)
    return [
        pltpu.VMEM((ng, GROUP_PAIRS, 2, 2 * BLK, kc), BF16),
        pltpu.VMEM((ng, GROUP_PAIRS, NSLOT, 2 * BLK, BLK), F32),
        pltpu.VMEM((ng, GROUP_PAIRS, 2, 2 * BLK, LANES), F32),
    ]


HEADS_PER_STEP = 2


class _Head:
    def __init__(self, **kw):
        self.__dict__.update(kw)


def _lanes(u, width=LANES):
    return slice(u * width, (u + 1) * width)


def _head_maps(nb, ncol):
    def cur(bi, hp):
        return bi, hp

    def nxt(bi, hp):
        flat = jnp.minimum((bi * (ncol // HEADS_PER_STEP) + hp + 1) * HEADS_PER_STEP, nb * ncol - 1)
        return flat // ncol, flat % ncol

    return cur, nxt


def _seq_spec(width, which, col0):
    def index_map(bi, hp):
        b2, h2 = which(bi, hp)
        return b2, 0, 0, col0 + h2
    return pl.BlockSpec((1, NBLK, BLK, width), index_map)


def _bias_spec(which, bias0, count):
    def index_map(bi, hp):
        _, h2 = which(bi, hp)
        return bias0 + h2, 0, 0, 0
    return pl.BlockSpec((count, 3, 2 * BLK, BLK), index_map)


def _diff_kernel(q2, q_nx, k2, k_nx, v2, g2, bias2, bias_nx, lam_ref, sg_ref, o_ref, *scr):
    qe_scr = scr[0]
    lane = lax.broadcasted_iota(jnp.int32, (BLK, LANES), 1)

    def build(q_at, g):
        for j, pa in enumerate(PAIR_GROUPS[g]):
            for t, qt in enumerate((pa, NBLK - 1 - pa)):
                q = q_at(qt).astype(F32)
                qe_scr[g, j, t, :BLK, :] = jnp.where(lane < DIFF_QK, q, 0.0).astype(BF16)
                qe_scr[g, j, t, BLK:, :] = jnp.where(lane >= DIFF_QK, q, 0.0).astype(BF16)

    def make_finish(u):
        def finish(qt, o):
            lp = lam_ref[...]
            lambda_init = 0.8 - 0.6 * math.exp(-0.3 * 0)
            lam = (jnp.exp(jnp.sum(lp[0:1] * lp[1:2], axis=-1, keepdims=True))
                   - jnp.exp(jnp.sum(lp[2:3] * lp[3:4], axis=-1, keepdims=True)) + lambda_init)
            out = o[:BLK] - lam * o[BLK:]
            ms = jnp.mean(out * out, axis=-1, keepdims=True)
            out = out * lax.rsqrt(ms + SUBLN_EPS) * sg_ref[...] * (1.0 - lambda_init)
            out = out * _silu(g2[0, qt][:, _lanes(u)].astype(F32))
            o_ref[qt * BLK:(qt + 1) * BLK, _lanes(u)] = out.astype(BF16)
        return finish

    def head(u):
        last = u == HEADS_PER_STEP - 1
        q_cur = lambda qt: q2[0, qt][:, _lanes(u)]
        q_nxt = (lambda qt: q_nx[0, qt]) if last else (lambda qt: q2[0, qt][:, _lanes(u + 1)])
        return _Head(
            build_cur=lambda g: build(q_cur, g), build_nxt=lambda g: build(q_nxt, g),
            k_cur=lambda kb: k2[0, kb][:, _lanes(u)],
            k_nxt=(lambda kb: k_nx[0, kb]) if last else (lambda kb: k2[0, kb][:, _lanes(u + 1)]),
            v_cur=lambda kb: _with_ones(v2[0, kb][:, _lanes(u)]),
            bias_cur=bias2.at[u:u + 1], bias_nxt=bias_nx if last else bias2.at[u + 1:u + 2],
            finish=make_finish(u))

    first = (pl.program_id(0) == 0) & (pl.program_id(1) == 0)
    _pipelined_step(first, [head(u) for u in range(HEADS_PER_STEP)], scr)


def _diff_attention(qkg4, kv4, bias, diff_lambda, subln_g):
    nb = qkg4.shape[0]
    grp = GROUP // LANES
    hps = HEADS_PER_STEP
    cur, nxt = _head_maps(nb, DIFF_HEADS)
    const = lambda bi, hp: (0, 0)
    return pl.pallas_call(
        _diff_kernel,
        grid=(nb, DIFF_HEADS // hps),
        in_specs=[_seq_spec(hps * LANES, cur, 0), _seq_spec(LANES, nxt, 0),
                  _seq_spec(hps * LANES, cur, grp // hps), _seq_spec(LANES, nxt, grp),
                  _seq_spec(hps * LANES, cur, grp // hps), _seq_spec(hps * LANES, cur, 2 * grp // hps),
                  _bias_spec(cur, 0, hps), _bias_spec(nxt, 0, 1),
                  pl.BlockSpec((4, DIFF_QK), const), pl.BlockSpec((1, LANES), const)],
        out_specs=pl.BlockSpec((NBLK * BLK, hps * LANES), lambda bi, hp: (bi, hp)),
        out_shape=jax.ShapeDtypeStruct((nb * NBLK * BLK, DIFF_HEADS * LANES), BF16),
        scratch_shapes=_core_scratch(LANES),
        compiler_params=pltpu.CompilerParams(
            dimension_semantics=("arbitrary", "arbitrary"), vmem_limit_bytes=VMEM_LIMIT),
        name="diff_attention",
    )(qkg4, qkg4, qkg4, qkg4, kv4, qkg4, bias, bias, diff_lambda, subln_g)


def _moba_kernel(q2, q_nx, k2, k_nx, v2, g2, bias2, bias_nx, kmean2, kmean_nx, o_ref, wg_scr, *scr):
    qe_scr = scr[0]
    lane = lax.broadcasted_iota(jnp.int32, (BLK, LANES), 1)
    n_iota = lax.broadcasted_iota(jnp.int32, (NBLK, BLK), 0)

    def gate_operand(kmean_at):
        wrow = lax.broadcasted_iota(jnp.int32, (LANES, LANES), 0)
        wlane = lax.broadcasted_iota(jnp.int32, (LANES, LANES), 1)
        wg = jnp.zeros((LANES, LANES), F32)
        for n in range(NBLK):
            kmean = kmean_at(n)
            hi = kmean.astype(BF16).astype(F32)
            mid = (kmean - hi).astype(BF16).astype(F32)
            lo = (kmean - hi - mid).astype(BF16).astype(F32)
            for piece, val in enumerate((hi, mid, lo)):
                for hh in range(2):
                    r = piece * 2 * NBLK + hh * NBLK + n
                    head_lanes = (wlane >= hh * MOBA_DIM) & (wlane < (hh + 1) * MOBA_DIM)
                    wg = jnp.where((wrow == r) & head_lanes, val, wg)
        return wg.astype(BF16)

    def build(q_at, wg, g):
        for j, pa in enumerate(PAIR_GROUPS[g]):
            for t, qt in enumerate((pa, NBLK - 1 - pa)):
                q = q_at(qt)
                qf = q.astype(F32)
                if qt == 0:
                    mask = jnp.zeros((BLK, LANES), F32)
                else:
                    g_t = lax.dot_general(wg, q, _NT, preferred_element_type=F32)
                    g_t = g_t[0:2 * NBLK] + g_t[2 * NBLK:4 * NBLK] + g_t[4 * NBLK:6 * NBLK]
                    mask_rows = []
                    for hh in range(2):
                        g8 = g_t[hh * NBLK:(hh + 1) * NBLK]
                        cnt = jnp.zeros((NBLK, BLK), jnp.int32)
                        for i in range(qt):
                            gi = g8[i:i + 1]
                            beats = (gi > g8) | ((gi == g8) & (i < n_iota))
                            cnt = cnt + jnp.where(beats, 1, 0)
                        keep = ((n_iota < qt) & (cnt < MOBA_TOPK)) | (n_iota == qt)
                        mask_rows.append(jnp.where(keep, 0.0, -MASK_BIG))
                    mask_t = jnp.concatenate(mask_rows + [jnp.zeros((LANES - 2 * NBLK, BLK), F32)], axis=0)
                    mask = mask_t.T
                for hh in range(2):
                    head_lanes = (lane >= hh * MOBA_DIM) & (lane < (hh + 1) * MOBA_DIM)
                    mask_lanes = (lane >= hh * NBLK) & (lane < (hh + 1) * NBLK)
                    qe_scr[g, j, t, hh * BLK:(hh + 1) * BLK, :LANES] = jnp.where(head_lanes, qf, 0.0).astype(BF16)
                    qe_scr[g, j, t, hh * BLK:(hh + 1) * BLK, LANES:] = jnp.where(mask_lanes, mask, 0.0).astype(BF16)

    def with_indicator(k, kb):
        onehot = jnp.where((lane == kb) | (lane == NBLK + kb), 1.0, 0.0).astype(BF16)
        return jnp.concatenate([k, onehot], axis=1)

    per_tile = kmean2.shape[1]

    def kmean_getter(ref, lanes):
        return lambda n: ref[n // per_tile, n % per_tile:n % per_tile + 1, lanes]

    def make_finish(u):
        def finish(qt, o):
            out = jnp.where(lane < MOBA_DIM, o[:BLK], o[BLK:])
            out = out * _silu(g2[0, qt][:, _lanes(u)].astype(F32))
            o_ref[qt * BLK:(qt + 1) * BLK, _lanes(u)] = out.astype(BF16)
        return finish

    def head(u):
        last = u == HEADS_PER_STEP - 1
        q_cur = lambda qt: q2[0, qt][:, _lanes(u)]
        q_nxt = (lambda qt: q_nx[0, qt]) if last else (lambda qt: q2[0, qt][:, _lanes(u + 1)])
        km_cur = kmean_getter(kmean2, _lanes(u))
        km_nxt = kmean_getter(kmean_nx, slice(None)) if last else kmean_getter(kmean2, _lanes(u + 1))

        def build_nxt(g):
            if g == 0:
                wg = gate_operand(km_nxt)
                wg_scr[...] = wg
            else:
                wg = wg_scr[...]
            build(q_nxt, wg, g)

        return _Head(
            build_cur=lambda g: build(q_cur, gate_operand(km_cur), g), build_nxt=build_nxt,
            k_cur=lambda kb: with_indicator(k2[0, kb][:, _lanes(u)], kb),
            k_nxt=(lambda kb: with_indicator(k_nx[0, kb], kb)) if last else (
                lambda kb: with_indicator(k2[0, kb][:, _lanes(u + 1)], kb)),
            v_cur=lambda kb: _with_ones(v2[0, kb][:, _lanes(u)]),
            bias_cur=bias2.at[u:u + 1], bias_nxt=bias_nx if last else bias2.at[u + 1:u + 2],
            finish=make_finish(u))

    first = (pl.program_id(0) == 0) & (pl.program_id(1) == 0)
    _pipelined_step(first, [head(u) for u in range(HEADS_PER_STEP)], scr)


def _moba_attention(qkg4, kv4, kmean, bias):
    nb = qkg4.shape[0]
    grp = GROUP // LANES
    pairs = MOBA_HEADS // 2
    hps = HEADS_PER_STEP
    cur, nxt = _head_maps(nb, pairs)
    tiles = kmean.shape[0] // nb

    def kmean_spec(which, width):
        def index_map(bi, hp):
            b2, h2 = which(bi, hp)
            return b2, 0, h2
        return pl.BlockSpec((tiles, kmean.shape[1], width), index_map)

    return pl.pallas_call(
        _moba_kernel,
        grid=(nb, pairs // hps),
        in_specs=[_seq_spec(hps * LANES, cur, 3 * grp // hps), _seq_spec(LANES, nxt, 3 * grp),
                  _seq_spec(hps * LANES, cur, 0), _seq_spec(LANES, nxt, 0),
                  _seq_spec(hps * LANES, cur, 2 * grp // hps), _seq_spec(hps * LANES, cur, 4 * grp // hps),
                  _bias_spec(cur, DIFF_HEADS // hps, hps), _bias_spec(nxt, DIFF_HEADS, 1),
                  kmean_spec(cur, hps * LANES), kmean_spec(nxt, LANES)],
        out_specs=pl.BlockSpec((NBLK * BLK, hps * LANES), lambda bi, hp: (bi, hp)),
        out_shape=jax.ShapeDtypeStruct((nb * NBLK * BLK, pairs * LANES), BF16),
        scratch_shapes=[pltpu.VMEM((LANES, LANES), BF16)] + _core_scratch(2 * LANES),
        compiler_params=pltpu.CompilerParams(
            dimension_semantics=("arbitrary", "arbitrary"), vmem_limit_bytes=VMEM_LIMIT),
        name="moba_attention",
    )(qkg4, qkg4, kv4, kv4, kv4, qkg4, bias, bias, kmean, kmean)


X_SLOTS = 3


def _outproj_kernel(a_ref, b_ref, x_hbm, g_ref, w_ref, o_ref, x_buf, x_sem):
    i = pl.program_id(0)
    nsteps = pl.num_programs(0)
    tm, half = a_ref.shape

    def x_copy(step):
        slot = step % X_SLOTS
        return pltpu.make_async_copy(x_hbm.at[pl.ds(step * tm, tm), :], x_buf.at[slot], x_sem.at[slot])

    @pl.when(i == 0)
    def _start_first():
        x_copy(0).start()

        @pl.when(nsteps > 1)
        def _():
            x_copy(1).start()

    @pl.when(i + 2 < nsteps)
    def _start_ahead():
        x_copy(i + 2).start()

    y = jnp.dot(a_ref[...], w_ref[:half, :], preferred_element_type=F32)
    y = y + jnp.dot(b_ref[...], w_ref[half:, :], preferred_element_type=F32)
    ms = jnp.mean(y * y, axis=-1, keepdims=True)
    y = y * lax.rsqrt(ms + NORM_EPS) * g_ref[...]
    x_copy(i).wait()
    o_ref[...] = x_buf[i % X_SLOTS] + y


def _outproj(a, b, x2d, g, w_bf16, tm=4 * BLK):
    m = x2d.shape[0]
    half = a.shape[1]
    return pl.pallas_call(
        _outproj_kernel,
        grid=(m // tm,),
        in_specs=[
            pl.BlockSpec((tm, half), lambda i: (i, 0)),
            pl.BlockSpec((tm, half), lambda i: (i, 0)),
            pl.BlockSpec(memory_space=pl.ANY),
            pl.BlockSpec((1, D_MODEL), lambda i: (0, 0)),
            pl.BlockSpec((2 * half, D_MODEL), lambda i: (0, 0)),
        ],
        out_specs=pl.BlockSpec((tm, D_MODEL), lambda i: (i, 0)),
        out_shape=jax.ShapeDtypeStruct((m, D_MODEL), F32),
        scratch_shapes=[pltpu.VMEM((X_SLOTS, tm, D_MODEL), F32), pltpu.SemaphoreType.DMA((X_SLOTS,))],
        compiler_params=pltpu.CompilerParams(
            dimension_semantics=("arbitrary",), vmem_limit_bytes=VMEM_LIMIT),
        name="outproj",
    )(a, b, x2d, g, w_bf16)


def kernel(x, norm_pre_g, w_in, diff_lambda, diff_subln_g, w_out, norm_post_g, rel_bias):
    b, s, d = x.shape
    assert d == D_MODEL and s == NBLK * BLK
    assert norm_pre_g.shape[0] == 1, "single-layer block"
    x2d = x.reshape(b * s, d)
    qkg, kv, kmean, bias = _inproj(x2d, norm_pre_g[0:1], w_in[0], rel_bias)
    qkg4 = qkg.reshape(b, NBLK, BLK, QKG_WIDTH)
    kv4 = kv.reshape(b, NBLK, BLK, KV_WIDTH)
    a_out = _diff_attention(qkg4, kv4, bias, diff_lambda[0], diff_subln_g[0:1])
    b_out = _moba_attention(qkg4, kv4, kmean, bias)
    out = _outproj(a_out, b_out, x2d, norm_post_g[0:1], w_out[0].astype(BF16))
    return out.reshape(b, s, d)
```

```python
import math

import numpy as np
import jax
import jax.numpy as jnp
from jax import lax
from jax.experimental import pallas as pl
from jax.experimental.pallas import tpu as pltpu

F32 = jnp.float32
BF16 = jnp.bfloat16

D_MODEL = 1024
DIFF_HEADS = 4
DIFF_QK = 64
MOBA_HEADS = 8
MOBA_DIM = 64
MOBA_TOPK = 3
N_BUCKETS = 32
MAX_DISTANCE = 128
IN_WIDTH = 4096
GROUP = 512
BLK = 256
NBLK = 8
NSLOT = NBLK + 1
PAIR_GROUPS = ((0, 1), (2, 3))
GROUP_PAIRS = 2
LANES = 128
NORM_EPS = 1e-6
SUBLN_EPS = 1e-5
QK_SCALE = 0.125
LOG2E = math.log2(math.e)
MASK_BIG = 1e30
VMEM_LIMIT = 56 * 1024 * 1024

QKG_WIDTH = 5 * GROUP
KV_WIDTH = 3 * GROUP

_NT = (((1,), (1,)), ((), ()))


def _rel_bucket(n):
    max_exact = N_BUCKETS // 2
    nf = np.maximum(n, 1).astype(np.float32)
    ratio = np.log(nf / np.float32(max_exact)) / np.float32(math.log(MAX_DISTANCE / max_exact))
    large = max_exact + (ratio * np.float32(N_BUCKETS - max_exact)).astype(np.int32)
    large = np.minimum(large, N_BUCKETS - 1)
    return np.where(n < max_exact, n, large).astype(np.int32)


def _bucket_vectors():
    l = np.arange(2 * BLK)
    out = []
    for t in range(2):
        d = t * BLK + BLK - 1 - l
        out.append(np.where(d >= 0, _rel_bucket(np.maximum(d, 0)), -1).astype(np.int32))
    return np.stack(out)[:, None, :]


N_BIAS_GROUPS = DIFF_HEADS + MOBA_HEADS // 2


def _bias_group(tab_ref, bkt_ref, out_ref, g):
    for half in range(2):
        head = jnp.where(g < DIFF_HEADS, g, 2 * g - DIFF_HEADS + half)
        for t in range(2):
            bk = bkt_ref[t]
            vec = jnp.full(bk.shape, -jnp.inf, F32)
            for b in range(N_BUCKETS):
                vec = jnp.where(bk == b, tab_ref[b, head] * LOG2E, vec)
            rows = jnp.broadcast_to(vec, (BLK, 2 * BLK))
            rolled = pltpu.roll(rows, 1, 1, stride=1, stride_axis=0)
            out_ref[0, t, half * BLK:(half + 1) * BLK, :] = rolled[:, BLK:]
        out_ref[0, 2, half * BLK:(half + 1) * BLK, :] = jnp.full((BLK, BLK), tab_ref[N_BUCKETS - 1, head] * LOG2E, F32)


def _inproj_kernel(tab_ref, bkt_ref, x_ref, g_ref, w_ref, qkg_ref, kv_ref, kmean_ref, bias_ref, wbf_scr):
    tm = x_ref.shape[0]

    @pl.when(pl.program_id(0) == 0)
    def _cast_weights():
        for j in range(IN_WIDTH // GROUP):
            wbf_scr[:, j * GROUP:(j + 1) * GROUP] = w_ref[:, j * GROUP:(j + 1) * GROUP].astype(BF16)

    x = x_ref[...]
    ms = jnp.mean(x * x, axis=-1, keepdims=True)
    h = (x * lax.rsqrt(ms + NORM_EPS) * g_ref[...]).astype(BF16)

    def project(j):
        return jnp.dot(h, wbf_scr[:, j * GROUP:(j + 1) * GROUP], preferred_element_type=F32)

    mk = project(5)
    kv_ref[:, 0 * GROUP:1 * GROUP] = mk.astype(BF16)
    for r in range(tm // BLK):
        kmean_ref[0, r:r + 1, :] = jnp.sum(mk[r * BLK:(r + 1) * BLK], axis=0, keepdims=True) * (1.0 / BLK)
    kv_ref[:, 1 * GROUP:2 * GROUP] = project(2).astype(BF16)
    kv_ref[:, 2 * GROUP:3 * GROUP] = project(6).astype(BF16)

    qkg_ref[:, 0 * GROUP:1 * GROUP] = (project(0) * (QK_SCALE * LOG2E)).astype(BF16)
    qkg_ref[:, 1 * GROUP:2 * GROUP] = project(1).astype(BF16)
    qkg_ref[:, 2 * GROUP:3 * GROUP] = project(3).astype(BF16)
    qkg_ref[:, 3 * GROUP:4 * GROUP] = (project(4) * (QK_SCALE * LOG2E)).astype(BF16)
    qkg_ref[:, 4 * GROUP:5 * GROUP] = project(7).astype(BF16)

    _bias_group(tab_ref, bkt_ref, bias_ref, pl.program_id(0) // (pl.num_programs(0) // N_BIAS_GROUPS))


def _inproj(x2d, g, w_f32, rel_bias, tm=4 * BLK):
    m = x2d.shape[0]
    row = lambda i: (i, 0)
    bkt = jnp.asarray(_bucket_vectors())
    return pl.pallas_call(
        _inproj_kernel,
        grid=(m // tm,),
        in_specs=[
            pl.BlockSpec(memory_space=pltpu.SMEM),
            pl.BlockSpec(bkt.shape, lambda i: (0, 0, 0)),
            pl.BlockSpec((tm, D_MODEL), row),
            pl.BlockSpec((1, D_MODEL), lambda i: (0, 0)),
            pl.BlockSpec((D_MODEL, IN_WIDTH), lambda i: (0, 0), pipeline_mode=pl.Buffered(1)),
        ],
        out_specs=[
            pl.BlockSpec((tm, QKG_WIDTH), row),
            pl.BlockSpec((tm, KV_WIDTH), row),
            pl.BlockSpec((1, tm // BLK, GROUP), lambda i: (i, 0, 0)),
            pl.BlockSpec((1, 3, 2 * BLK, BLK), lambda i: (i // (m // tm // N_BIAS_GROUPS), 0, 0, 0)),
        ],
        out_shape=[
            jax.ShapeDtypeStruct((m, QKG_WIDTH), BF16),
            jax.ShapeDtypeStruct((m, KV_WIDTH), BF16),
            jax.ShapeDtypeStruct((m // tm, tm // BLK, GROUP), F32),
            jax.ShapeDtypeStruct((N_BIAS_GROUPS, 3, 2 * BLK, BLK), F32),
        ],
        scratch_shapes=[pltpu.VMEM((D_MODEL, IN_WIDTH), BF16)],
        compiler_params=pltpu.CompilerParams(
            dimension_semantics=("arbitrary",), vmem_limit_bytes=VMEM_LIMIT),
        name="inproj",
    )(rel_bias, bkt, x2d, g, w_f32)


def _slot_plan(pa, k):
    if k <= pa:
        return 0, k, min(pa - k, 2)
    kb = k - pa - 1
    return 1, kb, min(NBLK - 1 - pa - kb, 2)


def _emit_streams(scr, finish, scores=None, weighted=None):
    qe_scr, s_scr, m_scr = scr
    for j in range(GROUP_PAIRS):
        m_run = [None, None]
        acc_run = [None, None]
        for k in range(NSLOT):
            if scores is not None:
                g, k_at, bias_ref = scores
                pa = PAIR_GROUPS[g][j]
                sel, kb, idx = _slot_plan(pa, k)
                s = lax.dot_general(qe_scr[g, j, sel], k_at(kb), _NT, preferred_element_type=F32)
                s = s + bias_ref[0, idx]
                s_scr[g, j, k] = s
                t = jnp.maximum(s[:, :LANES], s[:, LANES:])
                m_run[sel] = t if m_run[sel] is None else jnp.maximum(m_run[sel], t)
                if kb == (pa, NBLK - 1 - pa)[sel]:
                    m_row = jnp.max(m_run[sel], axis=-1, keepdims=True)
                    m_scr[g, j, sel] = jnp.broadcast_to(m_row, (2 * BLK, LANES))
            if weighted is not None:
                g, v_at = weighted
                pa = PAIR_GROUPS[g][j]
                sel, kb, _ = _slot_plan(pa, k)
                mb = m_scr[g, j, sel]
                s = s_scr[g, j, k]
                p = jnp.concatenate([jnp.exp2(s[:, :LANES] - mb), jnp.exp2(s[:, LANES:] - mb)], axis=1).astype(BF16)
                pv = jnp.dot(p, v_at(kb), preferred_element_type=F32)
                acc_run[sel] = pv if acc_run[sel] is None else acc_run[sel] + pv
                if kb == (pa, NBLK - 1 - pa)[sel]:
                    acc = acc_run[sel]
                    finish((pa, NBLK - 1 - pa)[sel], acc[:, :LANES] / acc[:, LANES:])


def _region(trips, fn):
    def body(i, carry):
        fn()
        return carry
    lax.fori_loop(0, trips, body, 0)


def _pipelined_step(first, heads, scr):
    one = jnp.minimum(pl.program_id(0) + 1, 1)
    h0 = heads[0]

    def prologue():
        h0.build_cur(0)
        h0.build_cur(1)

    def prologue_scores():
        _emit_streams(scr, h0.finish, scores=(0, h0.k_cur, h0.bias_cur))

    _region(jnp.where(first, 1, 0), prologue)
    _region(jnp.where(first, 1, 0), prologue_scores)
    for hd in heads:
        def region1(hd=hd):
            hd.build_nxt(0)
            _emit_streams(scr, hd.finish, scores=(1, hd.k_cur, hd.bias_cur), weighted=(0, hd.v_cur))

        def region2(hd=hd):
            hd.build_nxt(1)
            _emit_streams(scr, hd.finish, scores=(0, hd.k_nxt, hd.bias_nxt), weighted=(1, hd.v_cur))

        _region(one, region1)
        _region(one, region2)


def _with_ones(v):
    return jnp.concatenate([v, jnp.ones(v.shape, v.dtype)], axis=1)


def _silu(g):
    return g / (1.0 + jnp.exp(-g))


def _core_scratch(kc):
    ng = len(PAIR_GROUPS)
    return [
        pltpu.VMEM((ng, GROUP_PAIRS, 2, 2 * BLK, kc), BF16),
        pltpu.VMEM((ng, GROUP_PAIRS, NSLOT, 2 * BLK, BLK), F32),
        pltpu.VMEM((ng, GROUP_PAIRS, 2, 2 * BLK, LANES), F32),
    ]


HEADS_PER_STEP = 2


class _Head:
    def __init__(self, **kw):
        self.__dict__.update(kw)


def _lanes(u, width=LANES):
    return slice(u * width, (u + 1) * width)


def _head_maps(nb, ncol):
    def cur(bi, hp):
        return bi, hp

    def nxt(bi, hp):
        flat = jnp.minimum((bi * (ncol // HEADS_PER_STEP) + hp + 1) * HEADS_PER_STEP, nb * ncol - 1)
        return flat // ncol, flat % ncol

    return cur, nxt


def _seq_spec(width, which, col0):
    def index_map(bi, hp):
        b2, h2 = which(bi, hp)
        return b2, 0, 0, col0 + h2
    return pl.BlockSpec((1, NBLK, BLK, width), index_map)


def _bias_spec(which, bias0, count):
    def index_map(bi, hp):
        _, h2 = which(bi, hp)
        return bias0 + h2, 0, 0, 0
    return pl.BlockSpec((count, 3, 2 * BLK, BLK), index_map)


ALL_PAIRS = tuple(range(NBLK // 2))
DIFF_VMEM_LIMIT = 60 * 1024 * 1024


def _emit_head_streams(scr, finish, scores=None, weighted=None):
    qe_scr, s_scr, m_scr = scr
    for j, pa in enumerate(ALL_PAIRS):
        m_run = [None, None]
        acc_run = [None, None]
        for k in range(NSLOT):
            sel, kb, idx = _slot_plan(pa, k)
            if scores is not None:
                slot, k_at, bias_ref = scores
                s = lax.dot_general(qe_scr[slot, j, sel], k_at(kb), _NT, preferred_element_type=F32)
                s = s + bias_ref[0, idx]
                s_scr[slot, j, k] = s
                t = jnp.maximum(s[:, :LANES], s[:, LANES:])
                m_run[sel] = t if m_run[sel] is None else jnp.maximum(m_run[sel], t)
                if kb == (pa, NBLK - 1 - pa)[sel]:
                    m_row = jnp.max(m_run[sel], axis=-1, keepdims=True)
                    m_scr[slot, j, sel] = jnp.broadcast_to(m_row, (2 * BLK, LANES))
            if weighted is not None:
                slot, v_at = weighted
                mb = m_scr[slot, j, sel]
                s = s_scr[slot, j, k]
                p = jnp.concatenate([jnp.exp2(s[:, :LANES] - mb), jnp.exp2(s[:, LANES:] - mb)], axis=1).astype(BF16)
                pv = jnp.dot(p, v_at(kb), preferred_element_type=F32)
                acc_run[sel] = pv if acc_run[sel] is None else acc_run[sel] + pv
                if kb == (pa, NBLK - 1 - pa)[sel]:
                    acc = acc_run[sel]
                    finish((pa, NBLK - 1 - pa)[sel], acc[:, :LANES] / acc[:, LANES:])


def _diff_kernel(q_0, q_1, k_0, bias_0, q_n2, k_n1, bias_n1, v_cur, g_cur, lam_ref, sg_ref, o_ref, *scr):
    qe_scr = scr[0]
    lane = lax.broadcasted_iota(jnp.int32, (BLK, LANES), 1)
    step = pl.program_id(0) * pl.num_programs(1) + pl.program_id(1)
    par = step % 2
    first = step == 0

    def build(q_ref, slot):
        for j, pa in enumerate(ALL_PAIRS):
            for t, qt in enumerate((pa, NBLK - 1 - pa)):
                q = q_ref[0, qt].astype(F32)
                qe_scr[slot, j, t, :BLK, :] = jnp.where(lane < DIFF_QK, q, 0.0).astype(BF16)
                qe_scr[slot, j, t, BLK:, :] = jnp.where(lane >= DIFF_QK, q, 0.0).astype(BF16)

    def finish(qt, o):
        lp = lam_ref[...]
        lambda_init = 0.8 - 0.6 * math.exp(-0.3 * 0)
        lam = (jnp.exp(jnp.sum(lp[0:1] * lp[1:2], axis=-1, keepdims=True))
               - jnp.exp(jnp.sum(lp[2:3] * lp[3:4], axis=-1, keepdims=True)) + lambda_init)
        out = o[:BLK] - lam * o[BLK:]
        ms = jnp.mean(out * out, axis=-1, keepdims=True)
        out = out * lax.rsqrt(ms + SUBLN_EPS) * sg_ref[...] * (1.0 - lambda_init)
        out = out * _silu(g_cur[0, qt].astype(F32))
        o_ref[qt * BLK:(qt + 1) * BLK, :] = out.astype(BF16)

    def prologue_operands():
        build(q_0, 0)
        build(q_1, 1)

    def prologue_scores():
        _emit_head_streams(scr, finish, scores=(0, lambda kb: k_0[0, kb], bias_0))

    def steady(slot):
        def region():
            build(q_n2, slot)
            _emit_head_streams(scr, finish, scores=(1 - slot, lambda kb: k_n1[0, kb], bias_n1),
                               weighted=(slot, lambda kb: _with_ones(v_cur[0, kb])))
        return region

    _region(jnp.where(first, 1, 0), prologue_operands)
    _region(jnp.where(first, 1, 0), prologue_scores)
    for slot in range(2):
        _region(jnp.where(par == slot, 1, 0), steady(slot))


def _diff_attention(qkg4, kv4, bias, diff_lambda, subln_g):
    nb = qkg4.shape[0]
    grp = GROUP // LANES
    nheads = nb * DIFF_HEADS

    def ahead(d):
        def which(bi, h):
            flat = jnp.minimum(bi * DIFF_HEADS + h + d, nheads - 1)
            return flat // DIFF_HEADS, flat % DIFF_HEADS
        return which

    def fixed(head):
        return lambda bi, h: (0, head)

    once = dict(pipeline_mode=pl.Buffered(1))

    def seq(width, which, col0, **kw):
        def index_map(bi, h):
            b2, h2 = which(bi, h)
            return b2, 0, 0, col0 + h2
        return pl.BlockSpec((1, NBLK, BLK, width), index_map, **kw)

    def bias_spec(which, **kw):
        def index_map(bi, h):
            _, h2 = which(bi, h)
            return h2, 0, 0, 0
        return pl.BlockSpec((1, 3, 2 * BLK, BLK), index_map, **kw)

    const = lambda bi, h: (0, 0)
    return pl.pallas_call(
        _diff_kernel,
        grid=(nb, DIFF_HEADS),
        in_specs=[seq(LANES, fixed(0), 0, **once), seq(LANES, fixed(1), 0, **once),
                  seq(LANES, fixed(0), grp, **once), bias_spec(fixed(0), **once),
                  seq(LANES, ahead(2), 0), seq(LANES, ahead(1), grp), bias_spec(ahead(1)),
                  seq(LANES, ahead(0), grp), seq(LANES, ahead(0), 2 * grp),
                  pl.BlockSpec((4, DIFF_QK), const), pl.BlockSpec((1, LANES), const)],
        out_specs=pl.BlockSpec((NBLK * BLK, LANES), lambda bi, h: (bi, h)),
        out_shape=jax.ShapeDtypeStruct((nb * NBLK * BLK, DIFF_HEADS * LANES), BF16),
        scratch_shapes=[
            pltpu.VMEM((2, len(ALL_PAIRS), 2, 2 * BLK, LANES), BF16),
            pltpu.VMEM((2, len(ALL_PAIRS), NSLOT, 2 * BLK, BLK), F32),
            pltpu.VMEM((2, len(ALL_PAIRS), 2, 2 * BLK, LANES), F32),
        ],
        compiler_params=pltpu.CompilerParams(
            dimension_semantics=("arbitrary", "arbitrary"), vmem_limit_bytes=DIFF_VMEM_LIMIT),
        name="diff_attention",
    )(qkg4, qkg4, qkg4, bias, qkg4, qkg4, bias, kv4, qkg4, diff_lambda, subln_g)


def _moba_kernel(q2, q_nx, k2, k_nx, v2, g2, bias2, bias_nx, kmean2, kmean_nx, o_ref, wg_scr, *scr):
    qe_scr = scr[0]
    lane = lax.broadcasted_iota(jnp.int32, (BLK, LANES), 1)
    n_iota = lax.broadcasted_iota(jnp.int32, (NBLK, BLK), 0)

    def gate_operand(kmean_at):
        wrow = lax.broadcasted_iota(jnp.int32, (LANES, LANES), 0)
        wlane = lax.broadcasted_iota(jnp.int32, (LANES, LANES), 1)
        wg = jnp.zeros((LANES, LANES), F32)
        for n in range(NBLK):
            kmean = kmean_at(n)
            hi = kmean.astype(BF16).astype(F32)
            mid = (kmean - hi).astype(BF16).astype(F32)
            lo = (kmean - hi - mid).astype(BF16).astype(F32)
            for piece, val in enumerate((hi, mid, lo)):
                for hh in range(2):
                    r = piece * 2 * NBLK + hh * NBLK + n
                    head_lanes = (wlane >= hh * MOBA_DIM) & (wlane < (hh + 1) * MOBA_DIM)
                    wg = jnp.where((wrow == r) & head_lanes, val, wg)
        return wg.astype(BF16)

    def build(q_at, wg, g):
        for j, pa in enumerate(PAIR_GROUPS[g]):
            for t, qt in enumerate((pa, NBLK - 1 - pa)):
                q = q_at(qt)
                qf = q.astype(F32)
                if qt == 0:
                    mask = jnp.zeros((BLK, LANES), F32)
                else:
                    g_t = lax.dot_general(wg, q, _NT, preferred_element_type=F32)
                    g_t = g_t[0:2 * NBLK] + g_t[2 * NBLK:4 * NBLK] + g_t[4 * NBLK:6 * NBLK]
                    mask_rows = []
                    for hh in range(2):
                        g8 = g_t[hh * NBLK:(hh + 1) * NBLK]
                        cnt = jnp.zeros((NBLK, BLK), jnp.int32)
                        for i in range(qt):
                            gi = g8[i:i + 1]
                            beats = (gi > g8) | ((gi == g8) & (i < n_iota))
                            cnt = cnt + jnp.where(beats, 1, 0)
                        keep = ((n_iota < qt) & (cnt < MOBA_TOPK)) | (n_iota == qt)
                        mask_rows.append(jnp.where(keep, 0.0, -MASK_BIG))
                    mask_t = jnp.concatenate(mask_rows + [jnp.zeros((LANES - 2 * NBLK, BLK), F32)], axis=0)
                    mask = mask_t.T
                for hh in range(2):
                    head_lanes = (lane >= hh * MOBA_DIM) & (lane < (hh + 1) * MOBA_DIM)
                    mask_lanes = (lane >= hh * NBLK) & (lane < (hh + 1) * NBLK)
                    qe_scr[g, j, t, hh * BLK:(hh + 1) * BLK, :LANES] = jnp.where(head_lanes, qf, 0.0).astype(BF16)
                    qe_scr[g, j, t, hh * BLK:(hh + 1) * BLK, LANES:] = jnp.where(mask_lanes, mask, 0.0).astype(BF16)

    def with_indicator(k, kb):
        onehot = jnp.where((lane == kb) | (lane == NBLK + kb), 1.0, 0.0).astype(BF16)
        return jnp.concatenate([k, onehot], axis=1)

    per_tile = kmean2.shape[1]

    def kmean_getter(ref, lanes):
        return lambda n: ref[n // per_tile, n % per_tile:n % per_tile + 1, lanes]

    def make_finish(u):
        def finish(qt, o):
            out = jnp.where(lane < MOBA_DIM, o[:BLK], o[BLK:])
            out = out * _silu(g2[0, qt][:, _lanes(u)].astype(F32))
            o_ref[qt * BLK:(qt + 1) * BLK, _lanes(u)] = out.astype(BF16)
        return finish

    def head(u):
        last = u == HEADS_PER_STEP - 1
        q_cur = lambda qt: q2[0, qt][:, _lanes(u)]
        q_nxt = (lambda qt: q_nx[0, qt]) if last else (lambda qt: q2[0, qt][:, _lanes(u + 1)])
        km_cur = kmean_getter(kmean2, _lanes(u))
        km_nxt = kmean_getter(kmean_nx, slice(None)) if last else kmean_getter(kmean2, _lanes(u + 1))

        def build_nxt(g):
            if g == 0:
                wg = gate_operand(km_nxt)
                wg_scr[...] = wg
            else:
                wg = wg_scr[...]
            build(q_nxt, wg, g)

        return _Head(
            build_cur=lambda g: build(q_cur, gate_operand(km_cur), g), build_nxt=build_nxt,
            k_cur=lambda kb: with_indicator(k2[0, kb][:, _lanes(u)], kb),
            k_nxt=(lambda kb: with_indicator(k_nx[0, kb], kb)) if last else (
                lambda kb: with_indicator(k2[0, kb][:, _lanes(u + 1)], kb)),
            v_cur=lambda kb: _with_ones(v2[0, kb][:, _lanes(u)]),
            bias_cur=bias2.at[u:u + 1], bias_nxt=bias_nx if last else bias2.at[u + 1:u + 2],
            finish=make_finish(u))

    first = (pl.program_id(0) == 0) & (pl.program_id(1) == 0)
    _pipelined_step(first, [head(u) for u in range(HEADS_PER_STEP)], scr)


def _moba_attention(qkg4, kv4, kmean, bias):
    nb = qkg4.shape[0]
    grp = GROUP // LANES
    pairs = MOBA_HEADS // 2
    hps = HEADS_PER_STEP
    cur, nxt = _head_maps(nb, pairs)
    tiles = kmean.shape[0] // nb

    def kmean_spec(which, width):
        def index_map(bi, hp):
            b2, h2 = which(bi, hp)
            return b2, 0, h2
        return pl.BlockSpec((tiles, kmean.shape[1], width), index_map)

    return pl.pallas_call(
        _moba_kernel,
        grid=(nb, pairs // hps),
        in_specs=[_seq_spec(hps * LANES, cur, 3 * grp // hps), _seq_spec(LANES, nxt, 3 * grp),
                  _seq_spec(hps * LANES, cur, 0), _seq_spec(LANES, nxt, 0),
                  _seq_spec(hps * LANES, cur, 2 * grp // hps), _seq_spec(hps * LANES, cur, 4 * grp // hps),
                  _bias_spec(cur, DIFF_HEADS // hps, hps), _bias_spec(nxt, DIFF_HEADS, 1),
                  kmean_spec(cur, hps * LANES), kmean_spec(nxt, LANES)],
        out_specs=pl.BlockSpec((NBLK * BLK, hps * LANES), lambda bi, hp: (bi, hp)),
        out_shape=jax.ShapeDtypeStruct((nb * NBLK * BLK, pairs * LANES), BF16),
        scratch_shapes=[pltpu.VMEM((LANES, LANES), BF16)] + _core_scratch(2 * LANES),
        compiler_params=pltpu.CompilerParams(
            dimension_semantics=("arbitrary", "arbitrary"), vmem_limit_bytes=VMEM_LIMIT),
        name="moba_attention",
    )(qkg4, qkg4, kv4, kv4, kv4, qkg4, bias, bias, kmean, kmean)


X_SLOTS = 3


def _outproj_kernel(a_ref, b_ref, x_hbm, g_ref, w_ref, o_ref, x_buf, x_sem):
    i = pl.program_id(0)
    nsteps = pl.num_programs(0)
    tm, half = a_ref.shape

    def x_copy(step):
        slot = step % X_SLOTS
        return pltpu.make_async_copy(x_hbm.at[pl.ds(step * tm, tm), :], x_buf.at[slot], x_sem.at[slot])

    @pl.when(i == 0)
    def _start_first():
        x_copy(0).start()

        @pl.when(nsteps > 1)
        def _():
            x_copy(1).start()

    @pl.when(i + 2 < nsteps)
    def _start_ahead():
        x_copy(i + 2).start()

    y = jnp.dot(a_ref[...], w_ref[:half, :], preferred_element_type=F32)
    y = y + jnp.dot(b_ref[...], w_ref[half:, :], preferred_element_type=F32)
    ms = jnp.mean(y * y, axis=-1, keepdims=True)
    y = y * lax.rsqrt(ms + NORM_EPS) * g_ref[...]
    x_copy(i).wait()
    o_ref[...] = x_buf[i % X_SLOTS] + y


def _outproj(a, b, x2d, g, w_bf16, tm=4 * BLK):
    m = x2d.shape[0]
    half = a.shape[1]
    return pl.pallas_call(
        _outproj_kernel,
        grid=(m // tm,),
        in_specs=[
            pl.BlockSpec((tm, half), lambda i: (i, 0)),
            pl.BlockSpec((tm, half), lambda i: (i, 0)),
            pl.BlockSpec(memory_space=pl.ANY),
            pl.BlockSpec((1, D_MODEL), lambda i: (0, 0)),
            pl.BlockSpec((2 * half, D_MODEL), lambda i: (0, 0)),
        ],
        out_specs=pl.BlockSpec((tm, D_MODEL), lambda i: (i, 0)),
        out_shape=jax.ShapeDtypeStruct((m, D_MODEL), F32),
        scratch_shapes=[pltpu.VMEM((X_SLOTS, tm, D_MODEL), F32), pltpu.SemaphoreType.DMA((X_SLOTS,))],
        compiler_params=pltpu.CompilerParams(
            dimension_semantics=("arbitrary",), vmem_limit_bytes=VMEM_LIMIT),
        name="outproj",
    )(a, b, x2d, g, w_bf16)


def kernel(x, norm_pre_g, w_in, diff_lambda, diff_subln_g, w_out, norm_post_g, rel_bias):
    b, s, d = x.shape
    assert d == D_MODEL and s == NBLK * BLK
    assert norm_pre_g.shape[0] == 1, "single-layer block"
    x2d = x.reshape(b * s, d)
    qkg, kv, kmean, bias = _inproj(x2d, norm_pre_g[0:1], w_in[0], rel_bias)
    qkg4 = qkg.reshape(b, NBLK, BLK, QKG_WIDTH)
    kv4 = kv.reshape(b, NBLK, BLK, KV_WIDTH)
    a_out = _diff_attention(qkg4, kv4, bias, diff_lambda[0], diff_subln_g[0:1])
    b_out = _moba_attention(qkg4, kv4, kmean, bias)
    out = _outproj(a_out, b_out, x2d, norm_post_g[0:1], w_out[0].astype(BF16))
    return out.reshape(b, s, d)
```

```python
import math

import numpy as np
import jax
import jax.numpy as jnp
from jax import lax
from jax.experimental import pallas as pl
from jax.experimental.pallas import tpu as pltpu

F32 = jnp.float32
BF16 = jnp.bfloat16

D_MODEL = 1024
DIFF_HEADS = 4
DIFF_QK = 64
MOBA_HEADS = 8
MOBA_DIM = 64
MOBA_TOPK = 3
N_BUCKETS = 32
MAX_DISTANCE = 128
IN_WIDTH = 4096
GROUP = 512
BLK = 256
NBLK = 8
NSLOT = NBLK + 1
PAIR_GROUPS = ((0, 1), (2, 3))
GROUP_PAIRS = 2
LANES = 128
NORM_EPS = 1e-6
SUBLN_EPS = 1e-5
QK_SCALE = 0.125
LOG2E = math.log2(math.e)
MASK_BIG = 1e30
VMEM_LIMIT = 56 * 1024 * 1024

QKG_WIDTH = 5 * GROUP
KV_WIDTH = 3 * GROUP

_NT = (((1,), (1,)), ((), ()))


def _rel_bucket(n):
    max_exact = N_BUCKETS // 2
    nf = np.maximum(n, 1).astype(np.float32)
    ratio = np.log(nf / np.float32(max_exact)) / np.float32(math.log(MAX_DISTANCE / max_exact))
    large = max_exact + (ratio * np.float32(N_BUCKETS - max_exact)).astype(np.int32)
    large = np.minimum(large, N_BUCKETS - 1)
    return np.where(n < max_exact, n, large).astype(np.int32)


def _bucket_vectors():
    l = np.arange(2 * BLK)
    out = []
    for t in range(2):
        d = t * BLK + BLK - 1 - l
        out.append(np.where(d >= 0, _rel_bucket(np.maximum(d, 0)), -1).astype(np.int32))
    return np.stack(out)[:, None, :]


N_BIAS_GROUPS = DIFF_HEADS + MOBA_HEADS // 2


def _bias_group(tab_ref, bkt_ref, out_ref, g):
    for half in range(2):
        head = jnp.where(g < DIFF_HEADS, g, 2 * g - DIFF_HEADS + half)
        for t in range(2):
            bk = bkt_ref[t]
            vec = jnp.full(bk.shape, -jnp.inf, F32)
            for b in range(N_BUCKETS):
                vec = jnp.where(bk == b, tab_ref[b, head] * LOG2E, vec)
            rows = jnp.broadcast_to(vec, (BLK, 2 * BLK))
            rolled = pltpu.roll(rows, 1, 1, stride=1, stride_axis=0)
            out_ref[0, t, half * BLK:(half + 1) * BLK, :] = rolled[:, BLK:]
        out_ref[0, 2, half * BLK:(half + 1) * BLK, :] = jnp.full((BLK, BLK), tab_ref[N_BUCKETS - 1, head] * LOG2E, F32)


def _inproj_kernel(tab_ref, bkt_ref, x_ref, g_ref, w_ref, qkg_ref, kv_ref, kmean_ref, bias_ref, wbf_scr):
    tm = x_ref.shape[0]

    @pl.when(pl.program_id(0) == 0)
    def _cast_weights():
        for j in range(IN_WIDTH // GROUP):
            wbf_scr[:, j * GROUP:(j + 1) * GROUP] = w_ref[:, j * GROUP:(j + 1) * GROUP].astype(BF16)

    x = x_ref[...]
    ms = jnp.mean(x * x, axis=-1, keepdims=True)
    h = (x * lax.rsqrt(ms + NORM_EPS) * g_ref[...]).astype(BF16)

    def project(j):
        return jnp.dot(h, wbf_scr[:, j * GROUP:(j + 1) * GROUP], preferred_element_type=F32)

    mk = project(5)
    kv_ref[:, 0 * GROUP:1 * GROUP] = mk.astype(BF16)
    for r in range(tm // BLK):
        kmean_ref[0, r:r + 1, :] = jnp.sum(mk[r * BLK:(r + 1) * BLK], axis=0, keepdims=True) * (1.0 / BLK)
    kv_ref[:, 1 * GROUP:2 * GROUP] = project(2).astype(BF16)
    kv_ref[:, 2 * GROUP:3 * GROUP] = project(6).astype(BF16)

    qkg_ref[:, 0 * GROUP:1 * GROUP] = (project(0) * (QK_SCALE * LOG2E)).astype(BF16)
    qkg_ref[:, 1 * GROUP:2 * GROUP] = project(1).astype(BF16)
    qkg_ref[:, 2 * GROUP:3 * GROUP] = project(3).astype(BF16)
    qkg_ref[:, 3 * GROUP:4 * GROUP] = (project(4) * (QK_SCALE * LOG2E)).astype(BF16)
    qkg_ref[:, 4 * GROUP:5 * GROUP] = project(7).astype(BF16)

    _bias_group(tab_ref, bkt_ref, bias_ref, pl.program_id(0) // (pl.num_programs(0) // N_BIAS_GROUPS))


def _inproj(x2d, g, w_f32, rel_bias, tm=4 * BLK):
    m = x2d.shape[0]
    row = lambda i: (i, 0)
    bkt = jnp.asarray(_bucket_vectors())
    return pl.pallas_call(
        _inproj_kernel,
        grid=(m // tm,),
        in_specs=[
            pl.BlockSpec(memory_space=pltpu.SMEM),
            pl.BlockSpec(bkt.shape, lambda i: (0, 0, 0)),
            pl.BlockSpec((tm, D_MODEL), row),
            pl.BlockSpec((1, D_MODEL), lambda i: (0, 0)),
            pl.BlockSpec((D_MODEL, IN_WIDTH), lambda i: (0, 0), pipeline_mode=pl.Buffered(1)),
        ],
        out_specs=[
            pl.BlockSpec((tm, QKG_WIDTH), row),
            pl.BlockSpec((tm, KV_WIDTH), row),
            pl.BlockSpec((1, tm // BLK, GROUP), lambda i: (i, 0, 0)),
            pl.BlockSpec((1, 3, 2 * BLK, BLK), lambda i: (i // (m // tm // N_BIAS_GROUPS), 0, 0, 0)),
        ],
        out_shape=[
            jax.ShapeDtypeStruct((m, QKG_WIDTH), BF16),
            jax.ShapeDtypeStruct((m, KV_WIDTH), BF16),
            jax.ShapeDtypeStruct((m // tm, tm // BLK, GROUP), F32),
            jax.ShapeDtypeStruct((N_BIAS_GROUPS, 3, 2 * BLK, BLK), F32),
        ],
        scratch_shapes=[pltpu.VMEM((D_MODEL, IN_WIDTH), BF16)],
        compiler_params=pltpu.CompilerParams(
            dimension_semantics=("arbitrary",), vmem_limit_bytes=VMEM_LIMIT),
        name="inproj",
    )(rel_bias, bkt, x2d, g, w_f32)


def _slot_plan(pa, k):
    if k <= pa:
        return 0, k, min(pa - k, 2)
    kb = k - pa - 1
    return 1, kb, min(NBLK - 1 - pa - kb, 2)


def _emit_streams(scr, finish, scores=None, weighted=None):
    qe_scr, s_scr, m_scr = scr
    for j in range(GROUP_PAIRS):
        m_run = [None, None]
        acc_run = [None, None]
        for k in range(NSLOT):
            if scores is not None:
                g, k_at, bias_ref = scores
                pa = PAIR_GROUPS[g][j]
                sel, kb, idx = _slot_plan(pa, k)
                s = lax.dot_general(qe_scr[g, j, sel], k_at(kb), _NT, preferred_element_type=F32)
                s = s + bias_ref[0, idx]
                s_scr[g, j, k] = s
                t = jnp.maximum(s[:, :LANES], s[:, LANES:])
                m_run[sel] = t if m_run[sel] is None else jnp.maximum(m_run[sel], t)
                if kb == (pa, NBLK - 1 - pa)[sel]:
                    m_row = jnp.max(m_run[sel], axis=-1, keepdims=True)
                    m_scr[g, j, sel] = jnp.broadcast_to(m_row, (2 * BLK, LANES))
            if weighted is not None:
                g, v_at = weighted
                pa = PAIR_GROUPS[g][j]
                sel, kb, _ = _slot_plan(pa, k)
                mb = m_scr[g, j, sel]
                s = s_scr[g, j, k]
                p = jnp.concatenate([jnp.exp2(s[:, :LANES] - mb), jnp.exp2(s[:, LANES:] - mb)], axis=1).astype(BF16)
                pv = jnp.dot(p, v_at(kb), preferred_element_type=F32)
                acc_run[sel] = pv if acc_run[sel] is None else acc_run[sel] + pv
                if kb == (pa, NBLK - 1 - pa)[sel]:
                    acc = acc_run[sel]
                    finish((pa, NBLK - 1 - pa)[sel], acc[:, :LANES] / acc[:, LANES:])


def _region(trips, fn):
    def body(i, carry):
        fn()
        return carry
    lax.fori_loop(0, trips, body, 0)


def _pipelined_step(first, heads, scr):
    one = jnp.minimum(pl.program_id(0) + 1, 1)
    h0 = heads[0]

    def prologue():
        h0.build_cur(0)
        h0.build_cur(1)

    def prologue_scores():
        _emit_streams(scr, h0.finish, scores=(0, h0.k_cur, h0.bias_cur))

    _region(jnp.where(first, 1, 0), prologue)
    _region(jnp.where(first, 1, 0), prologue_scores)
    for hd in heads:
        def region1(hd=hd):
            hd.build_nxt(0)
            _emit_streams(scr, hd.finish, scores=(1, hd.k_cur, hd.bias_cur), weighted=(0, hd.v_cur))

        def region2(hd=hd):
            hd.build_nxt(1)
            _emit_streams(scr, hd.finish, scores=(0, hd.k_nxt, hd.bias_nxt), weighted=(1, hd.v_cur))

        _region(one, region1)
        _region(one, region2)


def _with_ones(v):
    return jnp.concatenate([v, jnp.ones(v.shape, v.dtype)], axis=1)


def _silu(g):
    return g / (1.0 + jnp.exp(-g))


def _core_scratch(kc):
    ng = len(PAIR_GROUPS)
    return [
        pltpu.VMEM((ng, GROUP_PAIRS, 2, 2 * BLK, kc), BF16),
        pltpu.VMEM((ng, GROUP_PAIRS, NSLOT, 2 * BLK, BLK), F32),
        pltpu.VMEM((ng, GROUP_PAIRS, 2, 2 * BLK, LANES), F32),
    ]


HEADS_PER_STEP = 2


class _Head:
    def __init__(self, **kw):
        self.__dict__.update(kw)


def _lanes(u, width=LANES):
    return slice(u * width, (u + 1) * width)


def _head_maps(nb, ncol):
    def cur(bi, hp):
        return bi, hp

    def nxt(bi, hp):
        flat = jnp.minimum((bi * (ncol // HEADS_PER_STEP) + hp + 1) * HEADS_PER_STEP, nb * ncol - 1)
        return flat // ncol, flat % ncol

    return cur, nxt


def _seq_spec(width, which, col0):
    def index_map(bi, hp):
        b2, h2 = which(bi, hp)
        return b2, 0, 0, col0 + h2
    return pl.BlockSpec((1, NBLK, BLK, width), index_map)


def _bias_spec(which, bias0, count):
    def index_map(bi, hp):
        _, h2 = which(bi, hp)
        return bias0 + h2, 0, 0, 0
    return pl.BlockSpec((count, 3, 2 * BLK, BLK), index_map)


ALL_PAIRS = tuple(range(NBLK // 2))
DIFF_VMEM_LIMIT = 60 * 1024 * 1024


def _emit_head_streams(scr, finish, scores=None, weighted=None):
    qe_scr, s_scr, m_scr = scr
    for j, pa in enumerate(ALL_PAIRS):
        m_run = [None, None]
        acc_run = [None, None]
        for k in range(NSLOT):
            sel, kb, idx = _slot_plan(pa, k)
            if scores is not None:
                slot, k_at, bias_ref = scores
                s = lax.dot_general(qe_scr[slot, j, sel], k_at(kb), _NT, preferred_element_type=F32)
                s = s + bias_ref[0, idx]
                s_scr[slot, j, k] = s
                t = jnp.maximum(s[:, :LANES], s[:, LANES:])
                m_run[sel] = t if m_run[sel] is None else jnp.maximum(m_run[sel], t)
                if kb == (pa, NBLK - 1 - pa)[sel]:
                    m_row = jnp.max(m_run[sel], axis=-1, keepdims=True)
                    m_scr[slot, j, sel] = jnp.broadcast_to(m_row, (2 * BLK, LANES))
            if weighted is not None:
                slot, v_at = weighted
                mb = m_scr[slot, j, sel]
                s = s_scr[slot, j, k]
                p = jnp.concatenate([jnp.exp2(s[:, :LANES] - mb), jnp.exp2(s[:, LANES:] - mb)], axis=1).astype(BF16)
                pv = jnp.dot(p, v_at(kb), preferred_element_type=F32)
                acc_run[sel] = pv if acc_run[sel] is None else acc_run[sel] + pv
                if kb == (pa, NBLK - 1 - pa)[sel]:
                    acc = acc_run[sel]
                    finish((pa, NBLK - 1 - pa)[sel], acc[:, :LANES] / acc[:, LANES:])


def _diff_kernel(q_0, q_1, k_0, bias_0, q_n2, k_n1, bias_n1, v_cur, g_cur, lam_ref, sg_ref, o_ref, *scr):
    qe_scr = scr[0]
    lane = lax.broadcasted_iota(jnp.int32, (BLK, LANES), 1)
    step = pl.program_id(0) * pl.num_programs(1) + pl.program_id(1)
    par = step % 2
    first = step == 0

    def build(q_ref, slot):
        for j, pa in enumerate(ALL_PAIRS):
            for t, qt in enumerate((pa, NBLK - 1 - pa)):
                q = q_ref[0, qt].astype(F32)
                qe_scr[slot, j, t, :BLK, :] = jnp.where(lane < DIFF_QK, q, 0.0).astype(BF16)
                qe_scr[slot, j, t, BLK:, :] = jnp.where(lane >= DIFF_QK, q, 0.0).astype(BF16)

    def finish(qt, o):
        lp = lam_ref[...]
        lambda_init = 0.8 - 0.6 * math.exp(-0.3 * 0)
        lam = (jnp.exp(jnp.sum(lp[0:1] * lp[1:2], axis=-1, keepdims=True))
               - jnp.exp(jnp.sum(lp[2:3] * lp[3:4], axis=-1, keepdims=True)) + lambda_init)
        out = o[:BLK] - lam * o[BLK:]
        ms = jnp.mean(out * out, axis=-1, keepdims=True)
        out = out * lax.rsqrt(ms + SUBLN_EPS) * sg_ref[...] * (1.0 - lambda_init)
        out = out * _silu(g_cur[0, qt].astype(F32))
        o_ref[qt * BLK:(qt + 1) * BLK, :] = out.astype(BF16)

    def prologue_operands():
        build(q_0, 0)
        build(q_1, 1)

    def prologue_scores():
        _emit_head_streams(scr, finish, scores=(0, lambda kb: k_0[0, kb], bias_0))

    def steady(slot):
        def region():
            build(q_n2, slot)
            _emit_head_streams(scr, finish, scores=(1 - slot, lambda kb: k_n1[0, kb], bias_n1),
                               weighted=(slot, lambda kb: _with_ones(v_cur[0, kb])))
        return region

    _region(jnp.where(first, 1, 0), prologue_operands)
    _region(jnp.where(first, 1, 0), prologue_scores)
    for slot in range(2):
        _region(jnp.where(par == slot, 1, 0), steady(slot))


def _diff_attention(qkg4, kv4, bias, diff_lambda, subln_g):
    nb = qkg4.shape[0]
    grp = GROUP // LANES
    nheads = nb * DIFF_HEADS

    def ahead(d):
        def which(bi, h):
            flat = jnp.minimum(bi * DIFF_HEADS + h + d, nheads - 1)
            return flat // DIFF_HEADS, flat % DIFF_HEADS
        return which

    def fixed(head):
        return lambda bi, h: (0, head)

    once = dict(pipeline_mode=pl.Buffered(1))

    def seq(width, which, col0, **kw):
        def index_map(bi, h):
            b2, h2 = which(bi, h)
            return b2, 0, 0, col0 + h2
        return pl.BlockSpec((1, NBLK, BLK, width), index_map, **kw)

    def bias_spec(which, **kw):
        def index_map(bi, h):
            _, h2 = which(bi, h)
            return h2, 0, 0, 0
        return pl.BlockSpec((1, 3, 2 * BLK, BLK), index_map, **kw)

    const = lambda bi, h: (0, 0)
    return pl.pallas_call(
        _diff_kernel,
        grid=(nb, DIFF_HEADS),
        in_specs=[seq(LANES, fixed(0), 0, **once), seq(LANES, fixed(1), 0, **once),
                  seq(LANES, fixed(0), grp, **once), bias_spec(fixed(0), **once),
                  seq(LANES, ahead(2), 0), seq(LANES, ahead(1), grp), bias_spec(ahead(1)),
                  seq(LANES, ahead(0), grp), seq(LANES, ahead(0), 2 * grp),
                  pl.BlockSpec((4, DIFF_QK), const), pl.BlockSpec((1, LANES), const)],
        out_specs=pl.BlockSpec((NBLK * BLK, LANES), lambda bi, h: (bi, h)),
        out_shape=jax.ShapeDtypeStruct((nb * NBLK * BLK, DIFF_HEADS * LANES), BF16),
        scratch_shapes=[
            pltpu.VMEM((2, len(ALL_PAIRS), 2, 2 * BLK, LANES), BF16),
            pltpu.VMEM((2, len(ALL_PAIRS), NSLOT, 2 * BLK, BLK), F32),
            pltpu.VMEM((2, len(ALL_PAIRS), 2, 2 * BLK, LANES), F32),
        ],
        compiler_params=pltpu.CompilerParams(
            dimension_semantics=("arbitrary", "arbitrary"), vmem_limit_bytes=DIFF_VMEM_LIMIT),
        name="diff_attention",
    )(qkg4, qkg4, qkg4, bias, qkg4, qkg4, bias, kv4, qkg4, diff_lambda, subln_g)


def _moba_kernel(q_0, q_1, k_0, bias_0, km_0, km_1, q_n2, km_n2, k_n1, bias_n1, v_cur, g_cur, o_ref, *scr):
    qe_scr = scr[0]
    lane = lax.broadcasted_iota(jnp.int32, (BLK, LANES), 1)
    n_iota = lax.broadcasted_iota(jnp.int32, (NBLK, BLK), 0)
    step = pl.program_id(0) * pl.num_programs(1) + pl.program_id(1)
    par = step % 2
    first = step == 0
    per_tile = km_0.shape[1]

    def kmean_getter(ref):
        return lambda n: ref[n // per_tile, n % per_tile:n % per_tile + 1, :]

    def gate_operand(kmean_at):
        wrow = lax.broadcasted_iota(jnp.int32, (LANES, LANES), 0)
        wlane = lax.broadcasted_iota(jnp.int32, (LANES, LANES), 1)
        wg = jnp.zeros((LANES, LANES), F32)
        for n in range(NBLK):
            kmean = kmean_at(n)
            hi = kmean.astype(BF16).astype(F32)
            mid = (kmean - hi).astype(BF16).astype(F32)
            lo = (kmean - hi - mid).astype(BF16).astype(F32)
            for piece, val in enumerate((hi, mid, lo)):
                for hh in range(2):
                    r = piece * 2 * NBLK + hh * NBLK + n
                    head_lanes = (wlane >= hh * MOBA_DIM) & (wlane < (hh + 1) * MOBA_DIM)
                    wg = jnp.where((wrow == r) & head_lanes, val, wg)
        return wg.astype(BF16)

    def build(q_ref, wg, slot):
        for j, pa in enumerate(ALL_PAIRS):
            for t, qt in enumerate((pa, NBLK - 1 - pa)):
                q = q_ref[0, qt]
                qf = q.astype(F32)
                if qt == 0:
                    mask = jnp.zeros((BLK, LANES), F32)
                else:
                    g_t = lax.dot_general(wg, q, _NT, preferred_element_type=F32)
                    g_t = g_t[0:2 * NBLK] + g_t[2 * NBLK:4 * NBLK] + g_t[4 * NBLK:6 * NBLK]
                    mask_rows = []
                    for hh in range(2):
                        g8 = g_t[hh * NBLK:(hh + 1) * NBLK]
                        cnt = jnp.zeros((NBLK, BLK), jnp.int32)
                        for i in range(qt):
                            gi = g8[i:i + 1]
                            beats = (gi > g8) | ((gi == g8) & (i < n_iota))
                            cnt = cnt + jnp.where(beats, 1, 0)
                        keep = ((n_iota < qt) & (cnt < MOBA_TOPK)) | (n_iota == qt)
                        mask_rows.append(jnp.where(keep, 0.0, -MASK_BIG))
                    mask_t = jnp.concatenate(mask_rows + [jnp.zeros((LANES - 2 * NBLK, BLK), F32)], axis=0)
                    mask = mask_t.T
                for hh in range(2):
                    head_lanes = (lane >= hh * MOBA_DIM) & (lane < (hh + 1) * MOBA_DIM)
                    mask_lanes = (lane >= hh * NBLK) & (lane < (hh + 1) * NBLK)
                    qe_scr[slot, j, t, hh * BLK:(hh + 1) * BLK, :LANES] = jnp.where(head_lanes, qf, 0.0).astype(BF16)
                    qe_scr[slot, j, t, hh * BLK:(hh + 1) * BLK, LANES:] = jnp.where(mask_lanes, mask, 0.0).astype(BF16)

    def with_indicator(k, kb):
        onehot = jnp.where((lane == kb) | (lane == NBLK + kb), 1.0, 0.0).astype(BF16)
        return jnp.concatenate([k, onehot], axis=1)

    def finish(qt, o):
        out = jnp.where(lane < MOBA_DIM, o[:BLK], o[BLK:])
        out = out * _silu(g_cur[0, qt].astype(F32))
        o_ref[qt * BLK:(qt + 1) * BLK, :] = out.astype(BF16)

    def prologue_operands():
        build(q_0, gate_operand(kmean_getter(km_0)), 0)
        build(q_1, gate_operand(kmean_getter(km_1)), 1)

    def prologue_scores():
        _emit_head_streams(scr, finish, scores=(0, lambda kb: with_indicator(k_0[0, kb], kb), bias_0))

    def steady(slot):
        def region():
            build(q_n2, gate_operand(kmean_getter(km_n2)), slot)
            _emit_head_streams(scr, finish, scores=(1 - slot, lambda kb: with_indicator(k_n1[0, kb], kb), bias_n1),
                               weighted=(slot, lambda kb: _with_ones(v_cur[0, kb])))
        return region

    _region(jnp.where(first, 1, 0), prologue_operands)
    _region(jnp.where(first, 1, 0), prologue_scores)
    for slot in range(2):
        _region(jnp.where(par == slot, 1, 0), steady(slot))


def _moba_attention(qkg4, kv4, kmean, bias):
    nb = qkg4.shape[0]
    grp = GROUP // LANES
    pairs = MOBA_HEADS // 2
    nheads = nb * pairs
    tiles = kmean.shape[0] // nb

    def ahead(d):
        def which(bi, h):
            flat = jnp.minimum(bi * pairs + h + d, nheads - 1)
            return flat // pairs, flat % pairs
        return which

    def fixed(head):
        return lambda bi, h: (0, head)

    once = dict(pipeline_mode=pl.Buffered(1))

    def seq(arr_col0, which, **kw):
        def index_map(bi, h):
            b2, h2 = which(bi, h)
            return b2, 0, 0, arr_col0 + h2
        return pl.BlockSpec((1, NBLK, BLK, LANES), index_map, **kw)

    def bias_spec(which, **kw):
        def index_map(bi, h):
            _, h2 = which(bi, h)
            return DIFF_HEADS + h2, 0, 0, 0
        return pl.BlockSpec((1, 3, 2 * BLK, BLK), index_map, **kw)

    def kmean_spec(which, **kw):
        def index_map(bi, h):
            b2, h2 = which(bi, h)
            return b2, 0, h2
        return pl.BlockSpec((tiles, kmean.shape[1], LANES), index_map, **kw)

    return pl.pallas_call(
        _moba_kernel,
        grid=(nb, pairs),
        in_specs=[seq(3 * grp, fixed(0), **once), seq(3 * grp, fixed(1), **once),
                  seq(0, fixed(0), **once), bias_spec(fixed(0), **once),
                  kmean_spec(fixed(0), **once), kmean_spec(fixed(1), **once),
                  seq(3 * grp, ahead(2)), kmean_spec(ahead(2)), seq(0, ahead(1)), bias_spec(ahead(1)),
                  seq(2 * grp, ahead(0)), seq(4 * grp, ahead(0))],
        out_specs=pl.BlockSpec((NBLK * BLK, LANES), lambda bi, h: (bi, h)),
        out_shape=jax.ShapeDtypeStruct((nb * NBLK * BLK, pairs * LANES), BF16),
        scratch_shapes=[
            pltpu.VMEM((2, len(ALL_PAIRS), 2, 2 * BLK, 2 * LANES), BF16),
            pltpu.VMEM((2, len(ALL_PAIRS), NSLOT, 2 * BLK, BLK), F32),
            pltpu.VMEM((2, len(ALL_PAIRS), 2, 2 * BLK, LANES), F32),
        ],
        compiler_params=pltpu.CompilerParams(
            dimension_semantics=("arbitrary", "arbitrary"), vmem_limit_bytes=DIFF_VMEM_LIMIT),
        name="moba_attention",
    )(qkg4, qkg4, kv4, bias, kmean, kmean, qkg4, kmean, kv4, bias, kv4, qkg4)


X_SLOTS = 3


def _outproj_kernel(a_ref, b_ref, x_hbm, g_ref, w_ref, o_ref, x_buf, x_sem):
    i = pl.program_id(0)
    nsteps = pl.num_programs(0)
    tm, half = a_ref.shape

    def x_copy(step):
        slot = step % X_SLOTS
        return pltpu.make_async_copy(x_hbm.at[pl.ds(step * tm, tm), :], x_buf.at[slot], x_sem.at[slot])

    @pl.when(i == 0)
    def _start_first():
        x_copy(0).start()

        @pl.when(nsteps > 1)
        def _():
            x_copy(1).start()

    @pl.when(i + 2 < nsteps)
    def _start_ahead():
        x_copy(i + 2).start()

    y = jnp.dot(a_ref[...], w_ref[:half, :], preferred_element_type=F32)
    y = y + jnp.dot(b_ref[...], w_ref[half:, :], preferred_element_type=F32)
    ms = jnp.mean(y * y, axis=-1, keepdims=True)
    y = y * lax.rsqrt(ms + NORM_EPS) * g_ref[...]
    x_copy(i).wait()
    o_ref[...] = x_buf[i % X_SLOTS] + y


def _outproj(a, b, x2d, g, w_bf16, tm=4 * BLK):
    m = x2d.shape[0]
    half = a.shape[1]
    return pl.pallas_call(
        _outproj_kernel,
        grid=(m // tm,),
        in_specs=[
            pl.BlockSpec((tm, half), lambda i: (i, 0)),
            pl.BlockSpec((tm, half), lambda i: (i, 0)),
            pl.BlockSpec(memory_space=pl.ANY),
            pl.BlockSpec((1, D_MODEL), lambda i: (0, 0)),
            pl.BlockSpec((2 * half, D_MODEL), lambda i: (0, 0)),
        ],
        out_specs=pl.BlockSpec((tm, D_MODEL), lambda i: (i, 0)),
        out_shape=jax.ShapeDtypeStruct((m, D_MODEL), F32),
        scratch_shapes=[pltpu.VMEM((X_SLOTS, tm, D_MODEL), F32), pltpu.SemaphoreType.DMA((X_SLOTS,))],
        compiler_params=pltpu.CompilerParams(
            dimension_semantics=("arbitrary",), vmem_limit_bytes=VMEM_LIMIT),
        name="outproj",
    )(a, b, x2d, g, w_bf16)


def kernel(x, norm_pre_g, w_in, diff_lambda, diff_subln_g, w_out, norm_post_g, rel_bias):
    b, s, d = x.shape
    assert d == D_MODEL and s == NBLK * BLK
    assert norm_pre_g.shape[0] == 1, "single-layer block"
    x2d = x.reshape(b * s, d)
    qkg, kv, kmean, bias = _inproj(x2d, norm_pre_g[0:1], w_in[0], rel_bias)
    qkg4 = qkg.reshape(b, NBLK, BLK, QKG_WIDTH)
    kv4 = kv.reshape(b, NBLK, BLK, KV_WIDTH)
    a_out = _diff_attention(qkg4, kv4, bias, diff_lambda[0], diff_subln_g[0:1])
    b_out = _moba_attention(qkg4, kv4, kmean, bias)
    out = _outproj(a_out, b_out, x2d, norm_post_g[0:1], w_out[0].astype(BF16))
    return out.reshape(b, s, d)
```
